```python
import jax, jax.numpy as jnp
from jax import lax
import numpy as np

D_MODEL = 1024
BATCH = 16
SEQ = 4096
DEPTH = 1
DEC_BATCH = 128
DEC_SEQ = 1
PAST_LEN = 8192
PAGE_SIZE = 128

MIX_WIDTH = D_MODEL
CONV_CH = MIX_WIDTH // 2
CONV_K = 31
HEAD_DIM = 64
N_HEADS = (MIX_WIDTH - CONV_CH) // HEAD_DIM
N_KV = 2
GQA_R = N_HEADS // N_KV
ROT_DIM = HEAD_DIM // 4
ROPE_THETA = 500000.0
CMP_LEN = 32
CMP_STRIDE = 16
CMP_HIDDEN = 2 * HEAD_DIM
SEL_BLOCK = 64
N_SEL = 16
WINDOW = 512
Q_BLOCK = 64
D_FF = 4 * D_MODEL
N_IN = 2 * CONV_CH + N_HEADS * HEAD_DIM + 6 * N_KV * HEAD_DIM + 3 * N_HEADS
EPS = 1e-6
NEG = -1e30

kernel_name = 'hymba_conformer_nsa_decode_step'


def rms_norm(x, g):
    xf = x.astype(jnp.float32)
    y = xf * lax.rsqrt(jnp.mean(xf * xf, -1, keepdims=True) + EPS)
    return (y * g.astype(jnp.float32)).astype(x.dtype)


def layer_norm(x, g, b):
    xf = x.astype(jnp.float32)
    xc = xf - jnp.mean(xf, -1, keepdims=True)
    y = xc * lax.rsqrt(jnp.mean(xc * xc, -1, keepdims=True) + EPS)
    return (y * g.astype(jnp.float32) + b.astype(jnp.float32)).astype(x.dtype)


def rope(x, pos):
    half = ROT_DIM // 2
    inv = jnp.power(jnp.float32(ROPE_THETA), -jnp.arange(half, dtype=jnp.float32) * 2.0 / ROT_DIM)
    ang = pos.astype(jnp.float32)[:, None] * inv[None, :]
    cos = jnp.cos(ang)[:, None, :]
    sin = jnp.sin(ang)[:, None, :]
    xf = x[..., :ROT_DIM].astype(jnp.float32)
    x1, x2 = xf[..., :half], xf[..., half:]
    rot = jnp.concatenate([x1 * cos - x2 * sin, x2 * cos + x1 * sin], -1).astype(x.dtype)
    return jnp.concatenate([rot, x[..., ROT_DIM:]], -1)


def masked_softmax(s, mask):
    s = jnp.where(mask, s.astype(jnp.float32), NEG)
    e = jnp.exp(s - jnp.max(s, -1, keepdims=True)) * mask
    return e / jnp.maximum(jnp.sum(e, -1, keepdims=True), 1e-30)


def compress(rows, w1, b1, w2, pe):
    B, T = rows.shape[:2]
    nch = T // CMP_STRIDE
    ch = rows[:, :nch * CMP_STRIDE].reshape(B, nch, CMP_STRIDE, N_KV, HEAD_DIM)
    first = jnp.einsum('bnjgd,jdh->bngh', ch, w1[:CMP_STRIDE])
    second = jnp.einsum('bnjgd,jdh->bngh', ch, w1[CMP_STRIDE:])
    bias = b1 + jnp.einsum('jd,jdh->h', pe, w1)
    h = jax.nn.gelu(first[:, :-1] + second[:, 1:] + bias)
    return jnp.einsum('bngh,hd->bngd', h, w2)


def overlap_matrix(n_cmp, n_blk):
    i = np.arange(n_cmp)[:, None]
    j = np.arange(n_blk)[None, :]
    lo = np.maximum(i * CMP_STRIDE, j * SEL_BLOCK)
    hi = np.minimum(i * CMP_STRIDE + CMP_LEN, (j + 1) * SEL_BLOCK)
    return jnp.asarray((np.maximum(hi - lo, 0) // CMP_STRIDE).astype(np.float32))


def cmp_and_select(q, q_pos, kc, vc, n_blk):
    n_cmp = kc.shape[1]
    s = jnp.einsum('btgrd,bngd->btgrn', q, kc).astype(jnp.float32) * (HEAD_DIM ** -0.5)
    c_end = jnp.arange(n_cmp) * CMP_STRIDE + (CMP_LEN - 1)
    vis = c_end[None, :] <= q_pos[:, None]
    p = masked_softmax(s, vis[None, :, None, None, :])
    o_c = jnp.einsum('btgrn,bngd->btgrd', p, vc.astype(jnp.float32))
    imp = jnp.einsum('btgn,nj->btgj', p.sum(3), overlap_matrix(n_cmp, n_blk))
    blk = jnp.arange(n_blk)[None, :]
    cur = (q_pos // SEL_BLOCK)[:, None]
    future = (blk > cur)[None, :, None, :]
    forced = ((blk == 0) | (blk == cur) | (blk == cur - 1))[None, :, None, :]
    score = jnp.where(future, -jnp.inf, jnp.where(forced, jnp.inf, imp))
    vals, idx = lax.top_k(score, min(N_SEL, n_blk))
    return o_c, idx, vals > -jnp.inf


def sel_branch(q, q_pos, ks, vs, idx, valid):
    B, Tq, G, n = idx.shape
    s = jnp.einsum('btgrd,btgnkd->btgrnk', q, ks).astype(jnp.float32) * (HEAD_DIM ** -0.5)
    kpos = idx[..., None] * SEL_BLOCK + jnp.arange(SEL_BLOCK)
    ok = (kpos <= q_pos[None, :, None, None, None]) & valid[..., None]
    p = masked_softmax(s.reshape(B, Tq, G, GQA_R, n * SEL_BLOCK), ok.reshape(B, Tq, G, 1, n * SEL_BLOCK))
    return jnp.einsum('btgrm,btgmd->btgrd', p, vs.reshape(B, Tq, G, n * SEL_BLOCK, HEAD_DIM).astype(jnp.float32))


def win_branch(q, q_pos, kw, vw, k_pos):
    s = jnp.einsum('btgrd,blgd->btgrl', q, kw).astype(jnp.float32) * (HEAD_DIM ** -0.5)
    kp = k_pos[None, :]
    qp = q_pos[:, None]
    ok = (kp <= qp) & (kp >= qp - WINDOW) & (kp >= 0)
    p = masked_softmax(s, ok[None, :, None, None, :])
    return jnp.einsum('btgrl,blgd->btgrd', p, vw.astype(jnp.float32))


def gate_combine(g, o_c, o_s, o_w):
    return g[:, :, 0, ..., None] * o_c + g[:, :, 1, ..., None] * o_s + g[:, :, 2, ..., None] * o_w


def nsa_prompt(q, k_cmp, v_cmp, k_sel, v_sel, k_win, v_win, gates, cmp_k, cmp_v):
    B, T = q.shape[:2]
    kc = compress(k_cmp, *cmp_k)
    vc = compress(v_cmp, *cmp_v)
    n_blk = T // SEL_BLOCK
    ks_b = k_sel.reshape(B, n_blk, SEL_BLOCK, N_KV, HEAD_DIM)
    vs_b = v_sel.reshape(B, n_blk, SEL_BLOCK, N_KV, HEAD_DIM)
    pad = ((0, 0), (WINDOW, 0), (0, 0), (0, 0))
    kw_pad = jnp.pad(k_win, pad)
    vw_pad = jnp.pad(v_win, pad)
    bidx = jnp.arange(B)[:, None, None, None]
    gidx = jnp.arange(N_KV)[None, None, :, None]

    def one_block(start):
        q_pos = start + jnp.arange(Q_BLOCK)
        q_c = lax.dynamic_slice_in_dim(q, start, Q_BLOCK, 1)
        g_c = lax.dynamic_slice_in_dim(gates, start, Q_BLOCK, 1)
        o_c, idx, valid = cmp_and_select(q_c, q_pos, kc, vc, n_blk)
        o_s = sel_branch(q_c, q_pos, ks_b[bidx, idx, :, gidx], vs_b[bidx, idx, :, gidx], idx, valid)
        kw = lax.dynamic_slice_in_dim(kw_pad, start, WINDOW + Q_BLOCK, 1)
        vw = lax.dynamic_slice_in_dim(vw_pad, start, WINDOW + Q_BLOCK, 1)
        k_pos = start - WINDOW + jnp.arange(WINDOW + Q_BLOCK)
        o_w = win_branch(q_c, q_pos, kw, vw, k_pos)
        return gate_combine(g_c, o_c, o_s, o_w).astype(q.dtype)

    out = lax.map(one_block, jnp.arange(T // Q_BLOCK) * Q_BLOCK)
    return jnp.moveaxis(out, 0, 1).reshape(B, T, N_HEADS * HEAD_DIM)


def nsa_sample(q, q_pos, k_cmp, v_cmp, k_sel, v_sel, k_win, v_win, gates, pool_k_cmp, pool_v_cmp,
               pool_k_sel, pool_v_sel, buf_k_win, buf_v_win, page_table, cmp_k, cmp_v):
    B, Tq = q.shape[:2]
    past = page_table.shape[1] * PAGE_SIZE

    def past_rows(pool):
        return pool[page_table].reshape(B, past, N_KV, HEAD_DIM)

    kc = compress(jnp.concatenate([past_rows(pool_k_cmp), k_cmp], 1), *cmp_k)
    vc = compress(jnp.concatenate([past_rows(pool_v_cmp), v_cmp], 1), *cmp_v)
    n_blk = -(-(past + Tq) // SEL_BLOCK)
    nb_past = past // SEL_BLOCK
    nb_new = n_blk - nb_past
    o_c, idx, valid = cmp_and_select(q, q_pos, kc, vc, n_blk)
    sub = PAGE_SIZE // SEL_BLOCK
    bidx = jnp.arange(B)[:, None, None, None]
    gidx = jnp.arange(N_KV)[None, None, :, None]
    jp = jnp.minimum(idx, nb_past - 1)
    page = page_table[bidx, jp // sub]
    off = jp % sub
    jn = jnp.clip(idx - nb_past, 0, nb_new - 1)
    from_past = (idx < nb_past)[..., None, None]

    def gather(pool, new):
        pool_b = pool.reshape(pool.shape[0], sub, SEL_BLOCK, N_KV, HEAD_DIM)
        new_b = jnp.pad(new, ((0, 0), (0, nb_new * SEL_BLOCK - Tq), (0, 0), (0, 0)))
        new_b = new_b.reshape(B, nb_new, SEL_BLOCK, N_KV, HEAD_DIM)
        return jnp.where(from_past, pool_b[page, off, :, gidx], new_b[bidx, jn, :, gidx])

    o_s = sel_branch(q, q_pos, gather(pool_k_sel, k_sel), gather(pool_v_sel, v_sel), idx, valid)
    wb = buf_k_win.shape[1]
    kw = jnp.concatenate([buf_k_win, k_win], 1)
    vw = jnp.concatenate([buf_v_win, v_win], 1)
    k_pos = past - wb + jnp.arange(wb + Tq)
    o_w = win_branch(q, q_pos, kw, vw, k_pos)
    o = gate_combine(gates, o_c, o_s, o_w).astype(q.dtype).reshape(B, Tq, N_HEADS * HEAD_DIM)
    return o, kw[:, -wb:], vw[:, -wb:]


def mixer_in(x, pos, g_pre, w_in):
    B, T, _ = x.shape
    z = rms_norm(x, g_pre) @ w_in
    c0 = 2 * CONV_CH
    c1 = c0 + N_HEADS * HEAD_DIM
    c2 = c1 + 6 * N_KV * HEAD_DIM
    u = z[..., :CONV_CH] * jax.nn.sigmoid(z[..., CONV_CH:c0])
    q = rope(z[..., c0:c1].reshape(B, T, N_HEADS, HEAD_DIM), pos).reshape(B, T, N_KV, GQA_R, HEAD_DIM)
    kv = z[..., c1:c2].reshape(B, T, 6, N_KV, HEAD_DIM)
    gates = jax.nn.sigmoid(z[..., c2:].astype(jnp.float32)).reshape(B, T, 3, N_KV, GQA_R)
    return (u, q, rope(kv[:, :, 0], pos), kv[:, :, 1], rope(kv[:, :, 2], pos), kv[:, :, 3],
            rope(kv[:, :, 4], pos), kv[:, :, 5], gates)


def dwconv(u_ext, w, b):
    y = lax.conv_general_dilated(u_ext, w[:, None, :].astype(u_ext.dtype), window_strides=(1,), padding='VALID',
                                 dimension_numbers=('NWC', 'WIO', 'NWC'), feature_group_count=CONV_CH)
    return y + b


def conv_tail(y, ln_g, ln_b):
    return jax.nn.silu(layer_norm(y, ln_g, ln_b))


def layer_out(x, conv_y, o_nsa, w_out, g_post, g_pre2, w_up, w_down, g_post2):
    mix = jnp.concatenate([conv_y, o_nsa.astype(conv_y.dtype)], -1) @ w_out
    h = x + rms_norm(mix, g_post)
    f = jnp.square(jax.nn.relu(rms_norm(h, g_pre2) @ w_up)) @ w_down
    return h + rms_norm(f, g_post2)


def setup_inputs(seed: int = 0) -> dict:
    key = jax.random.key(seed)
    ks = jax.random.split(key, 32)
    n_pages = PAST_LEN // PAGE_SIZE
    n_pool = (DEC_BATCH * n_pages * 5) // 4
    wb = min(WINDOW, PAST_LEN)

    def nrm(k, shape, s):
        return jax.random.normal(k, shape, jnp.float32) * s

    def gain(k, n):
        return 1.0 + nrm(k, (DEPTH, n), 0.05)

    pool = (DEPTH, n_pool, PAGE_SIZE, N_KV, HEAD_DIM)
    win = (DEPTH, DEC_BATCH, wb, N_KV, HEAD_DIM)
    page_table = jax.random.permutation(ks[9], n_pool)[:DEC_BATCH * n_pages]
    page_table = page_table.reshape(DEC_BATCH, n_pages).astype(jnp.int32)
    return {
        'x_prompt': nrm(ks[0], (BATCH, SEQ, D_MODEL), 1.0),
        'x_sample': nrm(ks[1], (DEC_BATCH, DEC_SEQ, D_MODEL), 1.0),
        'cache_k_cmp': nrm(ks[2], pool, 1.0),
        'cache_v_cmp': nrm(ks[3], pool, 1.0),
        'cache_k_sel': nrm(ks[4], pool, 1.0),
        'cache_v_sel': nrm(ks[5], pool, 1.0),
        'cache_k_win': nrm(ks[6], win, 1.0),
        'cache_v_win': nrm(ks[7], win, 1.0),
        'cache_conv': nrm(ks[8], (DEPTH, DEC_BATCH, CONV_K - 1, CONV_CH), 0.5),
        'page_table': page_table,
        'g_pre_mix': gain(ks[10], D_MODEL),
        'w_in': nrm(ks[11], (DEPTH, D_MODEL, N_IN), D_MODEL ** -0.5),
        'conv_w': nrm(ks[12], (DEPTH, CONV_K, CONV_CH), CONV_K ** -0.5),
        'conv_b': nrm(ks[13], (DEPTH, CONV_CH), 0.01),
        'conv_ln_g': gain(ks[14], CONV_CH),
        'conv_ln_b': nrm(ks[15], (DEPTH, CONV_CH), 0.01),
        'cmp_k_w1': nrm(ks[16], (DEPTH, CMP_LEN, HEAD_DIM, CMP_HIDDEN), (CMP_LEN * HEAD_DIM) ** -0.5),
        'cmp_k_b1': nrm(ks[17], (DEPTH, CMP_HIDDEN), 0.01),
        'cmp_k_w2': nrm(ks[18], (DEPTH, CMP_HIDDEN, HEAD_DIM), CMP_HIDDEN ** -0.5),
        'cmp_k_pe': nrm(ks[19], (DEPTH, CMP_LEN, HEAD_DIM), 0.1),
        'cmp_v_w1': nrm(ks[20], (DEPTH, CMP_LEN, HEAD_DIM, CMP_HIDDEN), (CMP_LEN * HEAD_DIM) ** -0.5),
        'cmp_v_b1': nrm(ks[21], (DEPTH, CMP_HIDDEN), 0.01),
        'cmp_v_w2': nrm(ks[22], (DEPTH, CMP_HIDDEN, HEAD_DIM), CMP_HIDDEN ** -0.5),
        'cmp_v_pe': nrm(ks[23], (DEPTH, CMP_LEN, HEAD_DIM), 0.1),
        'w_out': nrm(ks[24], (DEPTH, MIX_WIDTH, D_MODEL), MIX_WIDTH ** -0.5),
        'g_post_mix': gain(ks[25], D_MODEL),
        'g_pre_ffn': gain(ks[26], D_MODEL),
        'w_up': nrm(ks[27], (DEPTH, D_MODEL, D_FF), D_MODEL ** -0.5),
        'w_down': nrm(ks[28], (DEPTH, D_FF, D_MODEL), D_FF ** -0.5),
        'g_post_ffn': gain(ks[29], D_MODEL),
    }


def reference(x_prompt, x_sample, cache_k_cmp, cache_v_cmp, cache_k_sel, cache_v_sel, cache_k_win, cache_v_win,
              cache_conv, page_table, g_pre_mix, w_in, conv_w, conv_b, conv_ln_g, conv_ln_b,
              cmp_k_w1, cmp_k_b1, cmp_k_w2, cmp_k_pe, cmp_v_w1, cmp_v_b1, cmp_v_w2, cmp_v_pe,
              w_out, g_post_mix, g_pre_ffn, w_up, w_down, g_post_ffn):
    pos_p = jnp.arange(x_prompt.shape[1])
    pos_s = page_table.shape[1] * PAGE_SIZE + jnp.arange(x_sample.shape[1])
    hp, hs = x_prompt, x_sample
    st = [[] for _ in range(14)]
    for l in range(DEPTH):
        cmp_k = (cmp_k_w1[l], cmp_k_b1[l], cmp_k_w2[l], cmp_k_pe[l])
        cmp_v = (cmp_v_w1[l], cmp_v_b1[l], cmp_v_w2[l], cmp_v_pe[l])
        out_w = (w_out[l], g_post_mix[l], g_pre_ffn[l], w_up[l], w_down[l], g_post_ffn[l])
        u, q, kc, vc, ks, vs, kw, vw, gt = mixer_in(hp, pos_p, g_pre_mix[l], w_in[l])
        cy = conv_tail(dwconv(jnp.pad(u, ((0, 0), (CONV_K - 1, 0), (0, 0))), conv_w[l], conv_b[l]),
                       conv_ln_g[l], conv_ln_b[l])
        o = nsa_prompt(q, kc, vc, ks, vs, kw, vw, gt, cmp_k, cmp_v)
        wb = min(WINDOW, hp.shape[1])
        for i, a in enumerate((kc, vc, ks, vs, kw[:, -wb:], vw[:, -wb:], u[:, -(CONV_K - 1):])):
            st[i].append(a)
        hp = layer_out(hp, cy, o, *out_w)
        u, q, kc, vc, ks, vs, kw, vw, gt = mixer_in(hs, pos_s, g_pre_mix[l], w_in[l])
        u_ext = jnp.concatenate([cache_conv[l], u], 1)
        cy = conv_tail(dwconv(u_ext, conv_w[l], conv_b[l]), conv_ln_g[l], conv_ln_b[l])
        o, kw_buf, vw_buf = nsa_sample(q, pos_s, kc, vc, ks, vs, kw, vw, gt, cache_k_cmp[l], cache_v_cmp[l],
                                       cache_k_sel[l], cache_v_sel[l], cache_k_win[l], cache_v_win[l],
                                       page_table, cmp_k, cmp_v)
        for i, a in enumerate((kc, vc, ks, vs, kw_buf, vw_buf, u_ext[:, -(CONV_K - 1):])):
            st[7 + i].append(a)
        hs = layer_out(hs, cy, o, *out_w)
    s = [jnp.stack(a) for a in st]
    return (hp, hs, s[0], s[1], s[2], s[3], s[4], s[5], s[6], s[7], s[8], s[9], s[10], s[11], s[12], s[13])
```

```python
import functools

import numpy as np
import jax
import jax.numpy as jnp
from jax import lax
from jax.experimental import pallas as pl
from jax.experimental.pallas import tpu as pltpu

HEAD_DIM = 64
N_KV = 2
GQA_R = 4
N_HEADS = N_KV * GQA_R
CONV_K = 31
ROT_DIM = 16
ROPE_THETA = 500000.0
CMP_LEN = 32
CMP_STRIDE = 16
CMP_HIDDEN = 2 * HEAD_DIM
SEL_BLOCK = 64
N_SEL = 16
WINDOW = 512
PAGE_SIZE = 128
EPS = 1e-6
NEG = -1e30
SEL_OFF = -(2.0 ** 100)
LANES = 128
KV_W = N_KV * HEAD_DIM
N_GATE = 3 * N_HEADS
HALO = 32
SEL_CHUNK = 512
VMEM_LIMIT = 56 * 1024 * 1024

_MXU_DTYPE = jnp.bfloat16
F32 = jnp.float32


def _dot(a, b):
    return jnp.dot(a, b, preferred_element_type=F32)


def _dot_nt(a, b):
    return lax.dot_general(a, b, (((1,), (1,)), ((), ())), preferred_element_type=F32)


def _rms(x, g):
    return x * lax.rsqrt(jnp.mean(x * x, -1, keepdims=True) + EPS) * g


def _const_spec(shape):
    n = len(shape)
    return pl.BlockSpec(shape, lambda *_: (0,) * n, pipeline_mode=pl.Buffered(1))


def _params(sem):
    return pltpu.CompilerParams(dimension_semantics=sem, vmem_limit_bytes=VMEM_LIMIT)


def _inproj_kernel(x_ref, g_ref, w_ref, cos_ref, sa_ref, sb_ref, oh_ref,
                   u_ref, q_ref, kc_ref, vc_ref, ks_ref, vs_ref, kw_ref, vw_ref,
                   ksa_ref, vsb_ref, kwb_ref, vwb_ref, gate_ref):
    cc = u_ref.shape[1]
    xb = _rms(x_ref[...], g_ref[...]).astype(_MXU_DTYPE)
    cos, sa, sb = cos_ref[...], sa_ref[...], sb_ref[...]

    def seg(lo, hi):
        return _dot(xb, w_ref[:, lo:hi])

    def rope(v):
        return v * cos + pltpu.roll(v, LANES - ROT_DIM // 2, 1) * sa + pltpu.roll(v, ROT_DIM // 2, 1) * sb

    z = seg(0, 2 * cc)
    u_ref[...] = z[:, :cc] * jax.nn.sigmoid(z[:, cc:])
    c0 = 2 * cc
    zq = seg(c0, c0 + N_HEADS * HEAD_DIM)
    for c in range(GQA_R):
        q_ref[:, c * LANES:(c + 1) * LANES] = rope(zq[:, c * LANES:(c + 1) * LANES]).astype(q_ref.dtype)
    c1 = c0 + N_HEADS * HEAD_DIM
    zkv = seg(c1, c1 + 6 * KV_W)
    kc = rope(zkv[:, 0 * KV_W:1 * KV_W])
    vc = zkv[:, 1 * KV_W:2 * KV_W]
    ks = rope(zkv[:, 2 * KV_W:3 * KV_W])
    vs = zkv[:, 3 * KV_W:4 * KV_W]
    kw = rope(zkv[:, 4 * KV_W:5 * KV_W])
    vw = zkv[:, 5 * KV_W:6 * KV_W]
    kc_ref[...] = kc
    vc_ref[...] = vc
    ks_ref[...] = ks
    vs_ref[...] = vs
    kw_ref[...] = kw
    vw_ref[...] = vw
    ksa_ref[:, :KV_W] = ks.astype(ksa_ref.dtype)
    ksa_ref[:, KV_W:] = oh_ref[...]
    vsb_ref[...] = vs.astype(vsb_ref.dtype)
    kwb_ref[...] = kw.astype(kwb_ref.dtype)
    vwb_ref[...] = vw.astype(vwb_ref.dtype)
    c2 = c1 + 6 * KV_W
    gate_ref[...] = jax.nn.sigmoid(seg(c2, c2 + LANES))


def _rope_tables(pos):
    half = ROT_DIM // 2
    inv = jnp.power(jnp.float32(ROPE_THETA), -jnp.arange(half, dtype=jnp.float32) * 2.0 / ROT_DIM)
    ang = pos.astype(jnp.float32)[:, None] * inv[None, :]
    cos, sin = jnp.cos(ang), jnp.sin(ang)
    n = pos.shape[0]
    ones = jnp.ones((n, HEAD_DIM - ROT_DIM), F32)
    zeros8 = jnp.zeros((n, half), F32)
    zeros_rest = jnp.zeros((n, HEAD_DIM - ROT_DIM), F32)
    cos_h = jnp.concatenate([cos, cos, ones], 1)
    sa_h = jnp.concatenate([-sin, zeros8, zeros_rest], 1)
    sb_h = jnp.concatenate([zeros8, sin, zeros_rest], 1)
    two = lambda a: jnp.concatenate([a, a], 1)
    return two(cos_h), two(sa_h), two(sb_h)


def _block_onehot(pos):
    j = jnp.arange(LANES)[None, :]
    return ((pos[:, None] // SEL_BLOCK) == j).astype(_MXU_DTYPE)


def _inproj(x2d, pos, g_pre, w_cat, cc):
    n, d = x2d.shape
    period = pos.shape[0]
    tile = 512 if (n % 512 == 0 and period % 512 == 0) else period
    assert n % tile == 0 and period % tile == 0
    tpp = period // tile
    cos, sa, sb = _rope_tables(pos)
    oh = _block_onehot(pos)
    tok = lambda w: pl.BlockSpec((tile, w), lambda i: (i, 0))
    tab = lambda w: pl.BlockSpec((tile, w), lambda i: (i % tpp, 0))
    f32o = lambda w: jax.ShapeDtypeStruct((n, w), F32)
    b16o = lambda w: jax.ShapeDtypeStruct((n, w), _MXU_DTYPE)
    out_shape = (f32o(cc), b16o(N_HEADS * HEAD_DIM)) + (f32o(KV_W),) * 6 + (
        b16o(2 * KV_W), b16o(KV_W), b16o(KV_W), b16o(KV_W), f32o(LANES))
    out_specs = (tok(cc), tok(N_HEADS * HEAD_DIM)) + (tok(KV_W),) * 6 + (
        tok(2 * KV_W), tok(KV_W), tok(KV_W), tok(KV_W), tok(LANES))
    return pl.pallas_call(
        _inproj_kernel,
        grid=(n // tile,),
        in_specs=[tok(d), _const_spec((1, d)), _const_spec(w_cat.shape), tab(LANES), tab(LANES), tab(LANES),
                  tab(LANES)],
        out_specs=out_specs,
        out_shape=out_shape,
        compiler_params=_params(("parallel",)),
        name="inproj",
    )(x2d, g_pre.reshape(1, d), w_cat, cos, sa, sb, oh)


def _ln_silu(y, g, b):
    yc = y - jnp.mean(y, -1, keepdims=True)
    yn = yc * lax.rsqrt(jnp.mean(yc * yc, -1, keepdims=True) + EPS) * g + b
    return yn * jax.nn.sigmoid(yn)


def _conv_prompt_kernel(u_ref, halo_ref, w_ref, b_ref, g_ref, bb_ref, o_ref, ext_ref, *, rows):
    i = pl.program_id(1)
    tc = u_ref.shape[1]
    ext_ref[0:HALO, :] = jnp.where(i == 0, 0.0, halo_ref[0])
    ext_ref[HALO:, :] = u_ref[0]
    lead = HALO - (CONV_K - 1)

    for c in range(tc // rows):
        base = c * rows
        acc = jnp.zeros((rows, u_ref.shape[2]), F32)
        for k in range(CONV_K):
            acc = acc + ext_ref[base + lead + k:base + lead + k + rows, :] * w_ref[k:k + 1, :]
        y = _ln_silu(acc + b_ref[...], g_ref[...], bb_ref[...])
        o_ref[0, base:base + rows, :] = y.astype(o_ref.dtype)


def _conv_prompt(u, conv_w, conv_b, ln_g, ln_b):
    bsz, t, cc = u.shape
    tc = 512 if t % 512 == 0 else t
    rows = 64
    assert t % tc == 0 and tc % rows == 0 and tc % HALO == 0
    hb = tc // HALO
    vec = lambda a: a.reshape(1, cc)
    return pl.pallas_call(
        functools.partial(_conv_prompt_kernel, rows=rows),
        grid=(bsz, t // tc),
        in_specs=[pl.BlockSpec((1, tc, cc), lambda b, i: (b, i, 0)),
                  pl.BlockSpec((1, HALO, cc), lambda b, i: (b, jnp.maximum(i * hb - 1, 0), 0)),
                  _const_spec((CONV_K, cc)), _const_spec((1, cc)), _const_spec((1, cc)), _const_spec((1, cc))],
        out_specs=pl.BlockSpec((1, tc, cc), lambda b, i: (b, i, 0)),
        out_shape=jax.ShapeDtypeStruct((bsz, t, cc), _MXU_DTYPE),
        scratch_shapes=[pltpu.VMEM((tc + HALO, cc), F32)],
        compiler_params=_params(("parallel", "parallel")),
        name="conv_prompt",
    )(u, u, conv_w, vec(conv_b), vec(ln_g), vec(ln_b))


def _conv_sample_kernel(cache_ref, u_ref, w_ref, b_ref, g_ref, bb_ref, o_ref):
    nb = cache_ref.shape[0]
    w_hist = w_ref[0:CONV_K - 1, :]
    rows = [jnp.sum(cache_ref[i] * w_hist, axis=0, keepdims=True) for i in range(nb)]
    y = jnp.concatenate(rows, 0) + u_ref[...] * w_ref[CONV_K - 1:CONV_K, :] + b_ref[...]
    o_ref[...] = _ln_silu(y, g_ref[...], bb_ref[...])


def _conv_sample(cache_conv, u, conv_w, conv_b, ln_g, ln_b):
    nb, hist, cc = cache_conv.shape
    blk = 8
    assert nb % blk == 0 and hist == CONV_K - 1
    vec = lambda a: a.reshape(1, cc)
    return pl.pallas_call(
        _conv_sample_kernel,
        grid=(nb // blk,),
        in_specs=[pl.BlockSpec((blk, hist, cc), lambda i: (i, 0, 0)), pl.BlockSpec((blk, cc), lambda i: (i, 0)),
                  _const_spec((CONV_K, cc)), _const_spec((1, cc)), _const_spec((1, cc)), _const_spec((1, cc))],
        out_specs=pl.BlockSpec((blk, cc), lambda i: (i, 0)),
        out_shape=jax.ShapeDtypeStruct((nb, cc), F32),
        compiler_params=_params(("parallel",)),
        name="conv_sample",
    )(cache_conv, u, conv_w, vec(conv_b), vec(ln_g), vec(ln_b))


def _compress_rows(rows_ref, wbig_ref, bias_ref, w2_ref):
    nch = rows_ref.shape[0] // CMP_STRIDE
    acc = None
    for jj in range(CMP_STRIDE // 2):
        a = rows_ref[pl.ds(2 * jj, nch, stride=CMP_STRIDE), :]
        b = rows_ref[pl.ds(2 * jj + 1, nch, stride=CMP_STRIDE), :]
        d = _dot(jnp.concatenate([a, b], axis=1).astype(_MXU_DTYPE), wbig_ref[jj])
        acc = d if acc is None else acc + d
    hw = N_KV * CMP_HIDDEN
    first, second = acc[:, :hw], acc[:, hw:]
    second_next = pltpu.roll(second, nch - 1, 0)
    h = jax.nn.gelu(first + second_next + bias_ref[...], approximate=True)
    out = _dot(h.astype(_MXU_DTYPE), w2_ref[...])
    row = lax.broadcasted_iota(jnp.int32, out.shape, 0)
    return jnp.where(row < nch - 1, out, 0.0)


def _compress_weights(w1, b1, w2, pe):
    eye = jnp.eye(N_KV, dtype=F32)
    w1s = w1.reshape(2, CMP_STRIDE // 2, 2, HEAD_DIM, CMP_HIDDEN)
    wbig = jnp.einsum('sjldh,ge->jlgdseh', w1s, eye)
    wbig = wbig.reshape(CMP_STRIDE // 2, 2 * KV_W, 2 * N_KV * CMP_HIDDEN).astype(_MXU_DTYPE)
    bias = b1 + jnp.einsum('jd,jdh->h', pe, w1, precision=lax.Precision.HIGHEST)
    bias2 = jnp.tile(bias, N_KV).reshape(1, N_KV * CMP_HIDDEN)
    w2big = jnp.einsum('hd,ge->ghed', w2, eye).reshape(N_KV * CMP_HIDDEN, KV_W).astype(_MXU_DTYPE)
    return wbig, bias2, w2big


def _cmp_prompt_kernel(rk_ref, rv_ref, wk_ref, bk_ref, w2k_ref, wv_ref, bv_ref, w2v_ref, kc_ref, vc_ref):
    kc_ref[0] = _compress_rows(rk_ref.at[0], wk_ref, bk_ref, w2k_ref).astype(kc_ref.dtype)
    vc_ref[0] = _compress_rows(rv_ref.at[0], wv_ref, bv_ref, w2v_ref).astype(vc_ref.dtype)


def _cmp_prompt(rows_k, rows_v, wk, wv):
    bsz, t, _ = rows_k.shape
    nch = t // CMP_STRIDE
    assert t % CMP_STRIDE == 0
    wspecs = [_const_spec(a.shape) for a in wk]
    row_spec = pl.BlockSpec((1, t, KV_W), lambda b: (b, 0, 0))
    out_spec = pl.BlockSpec((1, nch, KV_W), lambda b: (b, 0, 0))
    out = jax.ShapeDtypeStruct((bsz, nch, KV_W), _MXU_DTYPE)
    return pl.pallas_call(
        _cmp_prompt_kernel,
        grid=(bsz,),
        in_specs=[row_spec, row_spec] + wspecs + wspecs,
        out_specs=(out_spec, out_spec),
        out_shape=(out, out),
        compiler_params=_params(("parallel",)),
        name="cmp_prompt",
    )(rows_k, rows_v, *wk, *wv)


def _page_copies(pt_ref, pool_ref, buf_ref, sem, seq, slot, n_pages, wait):
    for p in range(n_pages):
        page = 0 if wait else pt_ref[seq, p]
        cp = pltpu.make_async_copy(pool_ref.at[page], buf_ref.at[slot, pl.ds(p * PAGE_SIZE, PAGE_SIZE), :], sem)
        if wait:
            cp.wait()
        else:
            cp.start()


def _cmp_sample_kernel(pt_ref, pk_ref, pv_ref, wk_ref, bk_ref, w2k_ref, wv_ref, bv_ref, w2v_ref,
                       kc_ref, vc_ref, kbuf, vbuf, sem, *, n_pages):
    b = pl.program_id(0)
    slot = b % 2

    def gather(seq, sl, wait):
        _page_copies(pt_ref, pk_ref, kbuf, sem.at[0, sl], seq, sl, n_pages, wait)
        _page_copies(pt_ref, pv_ref, vbuf, sem.at[1, sl], seq, sl, n_pages, wait)

    @pl.when(b == 0)
    def _():
        gather(0, 0, False)

    @pl.when(b + 1 < pl.num_programs(0))
    def _():
        gather(b + 1, 1 - slot, False)

    gather(b, slot, True)
    kc_ref[0] = _compress_rows(kbuf.at[slot], wk_ref, bk_ref, w2k_ref).astype(kc_ref.dtype)
    vc_ref[0] = _compress_rows(vbuf.at[slot], wv_ref, bv_ref, w2v_ref).astype(vc_ref.dtype)


def _cmp_sample(page_table, pool_k, pool_v, wk, wv):
    nb, n_pages = page_table.shape
    past = n_pages * PAGE_SIZE
    nch = past // CMP_STRIDE
    wspecs = [pl.BlockSpec(a.shape, lambda b, pt, n=a.ndim: (0,) * n, pipeline_mode=pl.Buffered(1)) for a in wk]
    any_spec = pl.BlockSpec(memory_space=pl.ANY)
    out_spec = pl.BlockSpec((1, nch, KV_W), lambda b, pt: (b, 0, 0))
    out = jax.ShapeDtypeStruct((nb, nch, KV_W), _MXU_DTYPE)
    return pl.pallas_call(
        functools.partial(_cmp_sample_kernel, n_pages=n_pages),
        grid_spec=pltpu.PrefetchScalarGridSpec(
            num_scalar_prefetch=1,
            grid=(nb,),
            in_specs=[any_spec, any_spec] + wspecs + wspecs,
            out_specs=(out_spec, out_spec),
            scratch_shapes=[pltpu.VMEM((2, past, KV_W), F32), pltpu.VMEM((2, past, KV_W), F32),
                            pltpu.SemaphoreType.DMA((2, 2))]),
        out_shape=(out, out),
        compiler_params=_params(("arbitrary",)),
        name="cmp_sample",
    )(page_table, pool_k, pool_v, *wk, *wv)


def _gate_expand_matrix():
    e = np.zeros((2 * LANES, 3 * GQA_R * LANES), np.float32)
    for br in range(3):
        for g in range(N_KV):
            for r in range(GQA_R):
                c = br * N_HEADS + g * GQA_R + r
                lo = (br * GQA_R + r) * LANES + g * HEAD_DIM
                e[c, lo:lo + HEAD_DIM] = 1.0
                e[LANES + c, lo:lo + HEAD_DIM] = 1.0
    return jnp.asarray(e, dtype=_MXU_DTYPE)


def _expand_gates(gt, e_ref):
    hi = gt.astype(_MXU_DTYPE)
    lo = (gt - hi.astype(F32)).astype(_MXU_DTYPE)
    return _dot(jnp.concatenate([hi, lo], axis=1), e_ref[...])


def _overlap_t(n_cmp_pad, n_blk_pad, n_cmp, n_blk):
    i = np.arange(n_cmp_pad)[None, :]
    j = np.arange(n_blk_pad)[:, None]
    lo = np.maximum(i * CMP_STRIDE, j * SEL_BLOCK)
    hi = np.minimum(i * CMP_STRIDE + CMP_LEN, (j + 1) * SEL_BLOCK)
    ov = (np.maximum(hi - lo, 0) // CMP_STRIDE).astype(np.float32)
    ov = ov * (i < n_cmp) * (j < n_blk)
    return ov


def _nsa_prompt_kernel(q_ref, gate_ref, kc_ref, vc_ref, ksa_ref, vs_ref, kw_ref, vw_ref, ovt_ref, e_ref, o_ref,
                       m_ref, l_ref, acc_ref, *, seq_len):
    tq = q_ref.shape[1]
    rows = GQA_R * tq
    start = pl.program_id(1) * tq
    qt = q_ref[0]
    lane = lax.broadcasted_iota(jnp.int32, (tq, LANES), 1)
    gx = _expand_gates(gate_ref[0], e_ref)
    kc = kc_ref[0]
    vc = vc_ref[0]
    n_cmp = kc.shape[0]
    n_sel_rows = SEL_BLOCK

    def tok(shape):
        return start + (lax.broadcasted_iota(jnp.int32, shape, 0) & (tq - 1))

    def rep(v, n):
        return jnp.concatenate([v] * n, axis=1) if n > 1 else v

    wk = WINDOW + tq
    ws = pl.multiple_of(jnp.clip(start - WINDOW, 0, seq_len - wk), tq)
    n_chunks = (start + tq + SEL_CHUNK - 1) // SEL_CHUNK

    branch = [[None] * N_KV for _ in range(3)]
    for g in range(N_KV):
        mine = (lane < HEAD_DIM) if g == 0 else (lane >= HEAD_DIM)
        lhs = jnp.concatenate(
            [jnp.where(mine, qt[:, r * LANES:(r + 1) * LANES], jnp.zeros_like(qt[:, :LANES])) for r in range(GQA_R)],
            axis=0)

        s = _dot_nt(lhs, kc)
        vis = lax.broadcasted_iota(jnp.int32, s.shape, 1) * CMP_STRIDE + (CMP_LEN - 1) <= tok(s.shape)
        sm = jnp.where(vis, s, NEG)
        e = jnp.where(vis, jnp.exp(sm - jnp.max(sm, -1, keepdims=True)), 0.0)
        p = e * (1.0 / jnp.maximum(jnp.sum(e, -1, keepdims=True), 1e-30))
        branch[0][g] = _dot(p.astype(_MXU_DTYPE), vc)
        psum = (p[0:tq] + p[tq:2 * tq]) + p[2 * tq:3 * tq] + p[3 * tq:4 * tq]
        imp = _dot_nt(ovt_ref[...], psum.astype(_MXU_DTYPE))[:n_sel_rows]

        j = lax.broadcasted_iota(jnp.int32, imp.shape, 0)
        cur = (start + lax.broadcasted_iota(jnp.int32, imp.shape, 1)) // SEL_BLOCK
        future = j > cur
        forced = (j == 0) | (j == cur) | (j == cur - 1)
        score = jnp.where(future, -jnp.inf, jnp.where(forced, jnp.inf, imp))
        rank = jnp.zeros(imp.shape, jnp.int32)
        for i in range(n_sel_rows):
            si = score[i:i + 1, :]
            rank = rank + jnp.where(j > i, jnp.where(si >= score, 1, 0), jnp.where(si > score, 1, 0))
        bias_t = jnp.where(future, SEL_OFF, jnp.where(rank < N_SEL, 0.0, SEL_OFF))
        bias_t = jnp.concatenate([bias_t, jnp.zeros((LANES - n_sel_rows, tq), F32)], axis=0)
        bias = jnp.transpose(bias_t).astype(_MXU_DTYPE)
        lhs2 = jnp.concatenate([lhs, jnp.concatenate([bias] * GQA_R, axis=0)], axis=1)

        m_ref[...] = jnp.full(m_ref.shape, NEG, F32)
        l_ref[...] = jnp.zeros(l_ref.shape, F32)
        acc_ref[...] = jnp.zeros(acc_ref.shape, F32)

        def sel_step(c, carry, lhs2=lhs2):
            k0 = pl.multiple_of(c * SEL_CHUNK, SEL_CHUNK)
            s = _dot_nt(lhs2, ksa_ref[0, pl.ds(k0, SEL_CHUNK), :])
            kpos = k0 + lax.broadcasted_iota(jnp.int32, s.shape, 1)
            s = jnp.where(kpos <= tok(s.shape), s, NEG)
            m_prev = m_ref[...]
            m_new = jnp.maximum(m_prev, jnp.max(s, -1, keepdims=True))
            alpha = jnp.exp(m_prev - m_new)
            pe = jnp.exp(s - rep(m_new, SEL_CHUNK // LANES))
            l_ref[...] = alpha * l_ref[...] + jnp.sum(pe, -1, keepdims=True)
            acc_ref[...] = alpha * acc_ref[...] + _dot(pe.astype(_MXU_DTYPE), vs_ref[0, pl.ds(k0, SEL_CHUNK), :])
            m_ref[...] = m_new
            return carry

        lax.fori_loop(0, n_chunks, sel_step, 0)
        branch[1][g] = acc_ref[...] / l_ref[...]

        s = _dot_nt(lhs, kw_ref[0, pl.ds(ws, wk), :])
        kpos = ws + lax.broadcasted_iota(jnp.int32, s.shape, 1)
        t = tok(s.shape)
        s = jnp.where(kpos <= t, jnp.where(kpos >= t - WINDOW, s, NEG), NEG)
        e = jnp.exp(s - jnp.max(s, -1, keepdims=True))
        p = e * (1.0 / jnp.sum(e, -1, keepdims=True))
        branch[2][g] = _dot(p.astype(_MXU_DTYPE), vw_ref[0, pl.ds(ws, wk), :])

    first_half = lane < HEAD_DIM
    for r in range(GQA_R):
        out = jnp.zeros((tq, LANES), F32)
        for br in range(3):
            both = jnp.where(first_half, branch[br][0][r * tq:(r + 1) * tq], branch[br][1][r * tq:(r + 1) * tq])
            col = (br * GQA_R + r) * LANES
            out = out + gx[:, col:col + LANES] * both
        o_ref[0, :, r * LANES:(r + 1) * LANES] = out.astype(o_ref.dtype)


def _nsa_prompt(q, gates, kc, vc, ksa, vsb, kwb, vwb):
    bsz, t, qw = q.shape
    tq = 128
    n_cmp = kc.shape[1]
    assert t % SEL_CHUNK == 0 and t >= WINDOW + tq and t // SEL_BLOCK <= SEL_BLOCK and qw == GQA_R * LANES
    ovt = jnp.asarray(_overlap_t(n_cmp, LANES, n_cmp - 1, t // SEL_BLOCK), dtype=_MXU_DTYPE)
    e = _gate_expand_matrix()
    tile = lambda w: pl.BlockSpec((1, tq, w), lambda b, i: (b, i, 0))
    full = lambda n, w: pl.BlockSpec((1, n, w), lambda b, i: (b, 0, 0))
    rows = GQA_R * tq
    return pl.pallas_call(
        functools.partial(_nsa_prompt_kernel, seq_len=t),
        grid=(bsz, t // tq),
        in_specs=[tile(qw), tile(LANES), full(n_cmp, KV_W), full(n_cmp, KV_W), full(t, 2 * KV_W), full(t, KV_W),
                  full(t, KV_W), full(t, KV_W), _const_spec(ovt.shape), _const_spec(e.shape)],
        out_specs=tile(qw),
        out_shape=jax.ShapeDtypeStruct((bsz, t, qw), _MXU_DTYPE),
        scratch_shapes=[pltpu.VMEM((rows, LANES), F32)] * 3,
        compiler_params=_params(("parallel", "arbitrary")),
        name="nsa_prompt",
    )(q, gates, kc, vc, ksa, vsb, kwb, vwb, ovt, e)


def _query_rows(q4):
    q4 = q4.astype(F32)
    lane = lax.broadcasted_iota(jnp.int32, q4.shape, 1)
    return jnp.concatenate([jnp.where(lane < HEAD_DIM, q4, 0.0), jnp.where(lane >= HEAD_DIM, q4, 0.0)], axis=0)


def _sample_select_kernel(q_ref, kc_ref, vc_ref, ov_ref, oc_ref, idx_ref, *, n_vis, n_past_blk):
    lhs = _query_rows(q_ref[0]).astype(_MXU_DTYPE)
    s = _dot_nt(lhs, kc_ref[0])
    vis = lax.broadcasted_iota(jnp.int32, s.shape, 1) < n_vis
    sm = jnp.where(vis, s, NEG)
    e = jnp.where(vis, jnp.exp(sm - jnp.max(sm, -1, keepdims=True)), 0.0)
    p = e * (1.0 / jnp.maximum(jnp.sum(e, -1, keepdims=True), 1e-30))
    oc_ref[0] = _dot(p.astype(_MXU_DTYPE), vc_ref[0])
    row = lax.broadcasted_iota(jnp.int32, p.shape, 0)
    psum = jnp.concatenate([jnp.sum(jnp.where(row < GQA_R, p, 0.0), axis=0, keepdims=True),
                            jnp.sum(jnp.where(row >= GQA_R, p, 0.0), axis=0, keepdims=True)], axis=0)
    psum = jnp.concatenate([psum, jnp.zeros((8 - N_KV, psum.shape[1]), F32)], axis=0)
    imp = _dot(psum.astype(_MXU_DTYPE), ov_ref[...])
    j = lax.broadcasted_iota(jnp.int32, imp.shape, 1)
    score = jnp.where(j >= n_past_blk, -jnp.inf,
                      jnp.where(j == 0, jnp.inf, jnp.where(j == n_past_blk - 1, jnp.inf, imp)))
    picks = jnp.zeros(imp.shape, jnp.int32)
    for it in range(N_SEL - 1):
        best = jnp.max(score, -1, keepdims=True)
        pick = jnp.min(jnp.where(score == best, j, LANES), -1, keepdims=True)
        picks = jnp.where(j == it, pick, picks)
        score = jnp.where(j == pick, -jnp.inf, score)
    idx_ref[0] = picks


def _sample_select(q4, kc, vc, n_past_blk):
    nb = q4.shape[0]
    n_cmp_pad = kc.shape[1]
    assert n_past_blk <= LANES and n_past_blk >= N_SEL - 1
    ov = jnp.asarray(_overlap_t(n_cmp_pad, LANES, n_cmp_pad - 1, n_past_blk).T, dtype=_MXU_DTYPE)
    per = lambda n, w: pl.BlockSpec((1, n, w), lambda b: (b, 0, 0))
    return pl.pallas_call(
        functools.partial(_sample_select_kernel, n_vis=n_cmp_pad - 1, n_past_blk=n_past_blk),
        grid=(nb,),
        in_specs=[per(GQA_R, LANES), per(n_cmp_pad, KV_W), per(n_cmp_pad, KV_W), _const_spec(ov.shape)],
        out_specs=(per(8, LANES), per(8, LANES)),
        out_shape=(jax.ShapeDtypeStruct((nb, 8, LANES), F32), jax.ShapeDtypeStruct((nb, 8, LANES), jnp.int32)),
        compiler_params=_params(("parallel",)),
        name="sample_select",
    )(q4, kc, vc, ov)


def _block_copies(idx_ref, pt_ref, pool_ref, buf_ref, sem, seq, slot, wait):
    for g in range(N_KV):
        for i in range(N_SEL - 1):
            if wait:
                src = pool_ref.at[0, pl.ds(0, SEL_BLOCK), :]
            else:
                blk = idx_ref[seq * N_KV + g, i]
                page = pt_ref[seq, blk // (PAGE_SIZE // SEL_BLOCK)]
                off = pl.multiple_of((blk % (PAGE_SIZE // SEL_BLOCK)) * SEL_BLOCK, SEL_BLOCK)
                src = pool_ref.at[page, pl.ds(off, SEL_BLOCK), :]
            cp = pltpu.make_async_copy(src, buf_ref.at[slot, g, pl.ds(i * SEL_BLOCK, SEL_BLOCK), :], sem)
            if wait:
                cp.wait()
            else:
                cp.start()


def _attend_with_new(lhs, lhs32, keys, vals, k_new, v_new):
    top = lax.broadcasted_iota(jnp.int32, (8, 1), 0) < GQA_R
    per_group = isinstance(keys, (list, tuple))
    if per_group:
        s = jnp.where(top, _dot_nt(lhs, keys[0]), _dot_nt(lhs, keys[1]))
    else:
        s = _dot_nt(lhs, keys)
    kn = k_new.astype(_MXU_DTYPE).astype(F32)
    vn = v_new.astype(_MXU_DTYPE).astype(F32)
    s_new = jnp.sum(lhs32 * kn, -1, keepdims=True)
    m = jnp.maximum(jnp.max(s, -1, keepdims=True), s_new)
    e = jnp.exp(s - m)
    e_new = jnp.exp(s_new - m)
    eb = e.astype(_MXU_DTYPE)
    if per_group:
        pv = jnp.where(top, _dot(eb, vals[0]), _dot(eb, vals[1]))
    else:
        pv = _dot(eb, vals)
    return (pv + e_new * vn) / (jnp.sum(e, -1, keepdims=True) + e_new)


def _sample_attend_kernel(idx_ref, pt_ref, q_ref, gate_ref, oc_ref, ksn_ref, vsn_ref, kwn_ref, vwn_ref,
                          kwin_ref, vwin_ref, pks_ref, pvs_ref, e_ref, o_ref, kbuf, vbuf, sem):
    b = pl.program_id(0)
    slot = b % 2

    def gather(seq, sl, wait):
        _block_copies(idx_ref, pt_ref, pks_ref, kbuf, sem.at[0, sl], seq, sl, wait)
        _block_copies(idx_ref, pt_ref, pvs_ref, vbuf, sem.at[1, sl], seq, sl, wait)

    @pl.when(b == 0)
    def _():
        gather(0, 0, False)

    @pl.when(b + 1 < pl.num_programs(0))
    def _():
        gather(b + 1, 1 - slot, False)

    lhs32 = _query_rows(q_ref[0])
    lhs32 = lhs32.astype(_MXU_DTYPE).astype(F32)
    lhs = lhs32.astype(_MXU_DTYPE)
    o_w = _attend_with_new(lhs, lhs32, kwin_ref[0].astype(_MXU_DTYPE), vwin_ref[0].astype(_MXU_DTYPE),
                           kwn_ref[0], vwn_ref[0])
    gather(b, slot, True)
    keys = [kbuf[slot, g].astype(_MXU_DTYPE) for g in range(N_KV)]
    vals = [vbuf[slot, g].astype(_MXU_DTYPE) for g in range(N_KV)]
    o_s = _attend_with_new(lhs, lhs32, keys, vals, ksn_ref[0], vsn_ref[0])
    gx = _expand_gates(jnp.broadcast_to(gate_ref[0], (8, LANES)), e_ref)
    outs = (oc_ref[0], o_s, o_w)
    lane = lax.broadcasted_iota(jnp.int32, (1, LANES), 1)
    for r in range(GQA_R):
        out = jnp.zeros((1, LANES), F32)
        for br in range(3):
            both = jnp.where(lane < HEAD_DIM, outs[br][r:r + 1], outs[br][GQA_R + r:GQA_R + r + 1])
            col = (br * GQA_R + r) * LANES
            out = out + gx[0:1, col:col + LANES] * both
        o_ref[0, :, r * LANES:(r + 1) * LANES] = out


def _sample_attend(idx, page_table, q4, gates, o_c, ks_new, vs_new, kw_new, vw_new, buf_k_win, buf_v_win,
                   pool_k_sel, pool_v_sel):
    nb = q4.shape[0]
    wb = buf_k_win.shape[1]
    e = _gate_expand_matrix()
    n_keys = (N_SEL - 1) * SEL_BLOCK
    per = lambda n, w: pl.BlockSpec((1, n, w), lambda b, *_: (b, 0, 0))
    any_spec = pl.BlockSpec(memory_space=pl.ANY)
    e_spec = pl.BlockSpec(e.shape, lambda b, *_: (0, 0), pipeline_mode=pl.Buffered(1))
    row = lambda a: a.reshape(nb, 1, a.shape[-1])
    return pl.pallas_call(
        _sample_attend_kernel,
        grid_spec=pltpu.PrefetchScalarGridSpec(
            num_scalar_prefetch=2,
            grid=(nb,),
            in_specs=[per(GQA_R, LANES), per(1, LANES), per(8, LANES), per(1, KV_W), per(1, KV_W), per(1, KV_W),
                      per(1, KV_W), per(wb, KV_W), per(wb, KV_W), any_spec, any_spec, e_spec],
            out_specs=per(1, GQA_R * LANES),
            scratch_shapes=[pltpu.VMEM((2, N_KV, n_keys, KV_W), F32), pltpu.VMEM((2, N_KV, n_keys, KV_W), F32),
                            pltpu.SemaphoreType.DMA((2, 2))]),
        out_shape=jax.ShapeDtypeStruct((nb, 1, GQA_R * LANES), F32),
        compiler_params=_params(("arbitrary",)),
        name="sample_attend",
    )(idx, page_table, q4, row(gates), o_c, row(ks_new), row(vs_new), row(kw_new), row(vw_new),
      buf_k_win, buf_v_win, pool_k_sel, pool_v_sel, e)


def _layer_out_kernel(x_ref, cy_ref, o_ref, woc_ref, woo_ref, gpost_ref, gpre2_ref, wup_ref, wdn_ref, gpost2_ref,
                      y_ref, *, ff_chunk):
    mix = _dot(cy_ref[...], woc_ref[...]) + _dot(o_ref[...], woo_ref[...])
    h = x_ref[...] + _rms(mix, gpost_ref[...])
    hn = _rms(h, gpre2_ref[...]).astype(_MXU_DTYPE)
    f = None
    for c in range(wup_ref.shape[1] // ff_chunk):
        a = jnp.maximum(_dot(hn, wup_ref[:, c * ff_chunk:(c + 1) * ff_chunk]), 0.0)
        d = _dot((a * a).astype(_MXU_DTYPE), wdn_ref[c * ff_chunk:(c + 1) * ff_chunk, :])
        f = d if f is None else f + d
    y_ref[...] = h + _rms(f, gpost2_ref[...])


def _layer_out(x2d, cy, o, w_out_c, w_out_o, g_post, g_pre2, w_up, w_down, g_post2):
    n, d = x2d.shape
    tile = 512 if n % 512 == 0 else n
    tok = lambda w: pl.BlockSpec((tile, w), lambda i: (i, 0))
    vec = lambda a: a.reshape(1, d)
    return pl.pallas_call(
        functools.partial(_layer_out_kernel, ff_chunk=1024),
        grid=(n // tile,),
        in_specs=[tok(d), tok(cy.shape[1]), tok(o.shape[1]), _const_spec(w_out_c.shape), _const_spec(w_out_o.shape),
                  _const_spec((1, d)), _const_spec((1, d)), _const_spec(w_up.shape), _const_spec(w_down.shape),
                  _const_spec((1, d))],
        out_specs=tok(d),
        out_shape=jax.ShapeDtypeStruct((n, d), F32),
        compiler_params=_params(("parallel",)),
        name="layer_out",
    )(x2d, cy, o, w_out_c, w_out_o, vec(g_post), vec(g_pre2), w_up, w_down, vec(g_post2))


def _prep_w_in(w_in, cc):
    d = w_in.shape[0]
    c0 = 2 * cc
    c1 = c0 + N_HEADS * HEAD_DIM
    c2 = c1 + 6 * KV_W
    wq = w_in[:, c0:c1].reshape(d, N_KV, GQA_R, HEAD_DIM).transpose(0, 2, 1, 3).reshape(d, N_HEADS * HEAD_DIM)
    wq = wq * (HEAD_DIM ** -0.5)
    wg = jnp.pad(w_in[:, c2:], ((0, 0), (0, LANES - N_GATE)))
    return jnp.concatenate([w_in[:, :c0], wq, w_in[:, c1:c2], wg], axis=1).astype(_MXU_DTYPE)


def _prep_w_out(w_out, cc):
    d = w_out.shape[1]
    wo = w_out[cc:].reshape(N_KV, GQA_R, HEAD_DIM, d).transpose(1, 0, 2, 3).reshape(N_HEADS * HEAD_DIM, d)
    return w_out[:cc].astype(_MXU_DTYPE), wo.astype(_MXU_DTYPE)


def kernel(x_prompt, x_sample, cache_k_cmp, cache_v_cmp, cache_k_sel, cache_v_sel, cache_k_win, cache_v_win, cache_conv, page_table, g_pre_mix, w_in, conv_w, conv_b, conv_ln_g, conv_ln_b, cmp_k_w1, cmp_k_b1, cmp_k_w2, cmp_k_pe, cmp_v_w1, cmp_v_b1, cmp_v_w2, cmp_v_pe, w_out, g_post_mix, g_pre_ffn, w_up, w_down, g_post_ffn):
    bsz, t, d = x_prompt.shape
    nb, ts, _ = x_sample.shape
    depth = w_in.shape[0]
    cc = conv_w.shape[2]
    n_pages = page_table.shape[1]
    past = n_pages * PAGE_SIZE
    wb = cache_k_win.shape[2]
    assert ts == 1 and depth == 1 and KV_W == LANES and past % SEL_BLOCK == 0 and wb == min(WINDOW, past)
    assert min(WINDOW, t) == WINDOW
    l = 0
    pos_p = jnp.arange(t)
    pos_s = jnp.full((nb,), past, jnp.int32)

    w_cat = _prep_w_in(w_in[l], cc)
    w_out_c, w_out_o = _prep_w_out(w_out[l], cc)
    w_up_b = w_up[l].astype(_MXU_DTYPE)
    w_down_b = w_down[l].astype(_MXU_DTYPE)
    wk = _compress_weights(cmp_k_w1[l], cmp_k_b1[l], cmp_k_w2[l], cmp_k_pe[l])
    wv = _compress_weights(cmp_v_w1[l], cmp_v_b1[l], cmp_v_w2[l], cmp_v_pe[l])
    out_w = (w_out_c, w_out_o, g_post_mix[l], g_pre_ffn[l], w_up_b, w_down_b, g_post_ffn[l])
    kv5 = lambda a, n, m: a.reshape(1, n, m, N_KV, HEAD_DIM)

    (u, q, kc, vc, ks, vs, kw, vw, ksa, vsb, kwb, vwb, gates) = _inproj(
        x_prompt.reshape(bsz * t, d), pos_p, g_pre_mix[l], w_cat, cc)
    u3 = u.reshape(bsz, t, cc)
    cy = _conv_prompt(u3, conv_w[l], conv_b[l], conv_ln_g[l], conv_ln_b[l])
    b3 = lambda a: a.reshape(bsz, t, a.shape[-1])
    kcc, vcc = _cmp_prompt(b3(kc), b3(vc), wk, wv)
    o = _nsa_prompt(b3(q), b3(gates), kcc, vcc, b3(ksa), b3(vsb), b3(kwb), b3(vwb))
    y_p = _layer_out(x_prompt.reshape(bsz * t, d), cy.reshape(bsz * t, cc), o.reshape(bsz * t, -1), *out_w)
    y_p = y_p.reshape(bsz, t, d)
    p_states = (kv5(kc, bsz, t), kv5(vc, bsz, t), kv5(ks, bsz, t), kv5(vs, bsz, t),
                kv5(b3(kw)[:, -WINDOW:], bsz, WINDOW), kv5(b3(vw)[:, -WINDOW:], bsz, WINDOW),
                u3[:, -(CONV_K - 1):][None])

    (u_s, q_s, kc_s, vc_s, ks_s, vs_s, kw_s, vw_s, _, _, _, _, gates_s) = _inproj(
        x_sample.reshape(nb, d), pos_s, g_pre_mix[l], w_cat, cc)
    cy_s = _conv_sample(cache_conv[l], u_s, conv_w[l], conv_b[l], conv_ln_g[l], conv_ln_b[l])
    pool = lambda a: a[l].reshape(a.shape[1], PAGE_SIZE, KV_W)
    kcs, vcs = _cmp_sample(page_table, pool(cache_k_cmp), pool(cache_v_cmp), wk, wv)
    q4 = q_s.reshape(nb, GQA_R, LANES)
    o_c, picks = _sample_select(q4, kcs, vcs, past // SEL_BLOCK)
    idx = picks[:, :N_KV, :N_SEL].reshape(nb * N_KV, N_SEL)
    buf_k = cache_k_win[l].reshape(nb, wb, KV_W)
    buf_v = cache_v_win[l].reshape(nb, wb, KV_W)
    o_s = _sample_attend(idx, page_table, q4, gates_s, o_c, ks_s, vs_s, kw_s, vw_s, buf_k, buf_v,
                         pool(cache_k_sel), pool(cache_v_sel))
    y_s = _layer_out(x_sample.reshape(nb, d), cy_s.astype(_MXU_DTYPE), o_s.reshape(nb, -1).astype(_MXU_DTYPE), *out_w)
    y_s = y_s.reshape(nb, 1, d)
    new_win = lambda buf, new: jnp.concatenate([buf, new[:, None, :]], 1)[:, -wb:].reshape(1, nb, wb, N_KV, HEAD_DIM)
    s_conv = jnp.concatenate([cache_conv[l], u_s[:, None, :]], 1)[:, -(CONV_K - 1):][None]
    s_states = (kv5(kc_s, nb, 1), kv5(vc_s, nb, 1), kv5(ks_s, nb, 1), kv5(vs_s, nb, 1),
                new_win(buf_k, kw_s), new_win(buf_v, vw_s), s_conv)
    return (y_p, y_s) + p_states + s_states
```

```python
import functools

import numpy as np
import jax
import jax.numpy as jnp
from jax import lax
from jax.experimental import pallas as pl
from jax.experimental.pallas import tpu as pltpu

HEAD_DIM = 64
N_KV = 2
GQA_R = 4
N_HEADS = N_KV * GQA_R
CONV_K = 31
ROT_DIM = 16
ROPE_THETA = 500000.0
CMP_LEN = 32
CMP_STRIDE = 16
CMP_HIDDEN = 2 * HEAD_DIM
SEL_BLOCK = 64
N_SEL = 16
WINDOW = 512
PAGE_SIZE = 128
EPS = 1e-6
NEG = -1e30
SEL_OFF = -(2.0 ** 100)
LOG2E = 1.4426950408889634
LANES = 128
KV_W = N_KV * HEAD_DIM
N_GATE = 3 * N_HEADS
HALO = 32
SEL_CHUNK = 512
VMEM_LIMIT = 56 * 1024 * 1024

_MXU_DTYPE = jnp.bfloat16
F32 = jnp.float32


def _dot(a, b):
    return jnp.dot(a, b, preferred_element_type=F32)


def _dot_nt(a, b):
    return lax.dot_general(a, b, (((1,), (1,)), ((), ())), preferred_element_type=F32)


def _rms(x, g):
    return x * lax.rsqrt(jnp.mean(x * x, -1, keepdims=True) + EPS) * g


def _const_spec(shape):
    n = len(shape)
    return pl.BlockSpec(shape, lambda *_: (0,) * n, pipeline_mode=pl.Buffered(1))


def _params(sem):
    return pltpu.CompilerParams(dimension_semantics=sem, vmem_limit_bytes=VMEM_LIMIT)


def _inproj_kernel(x_ref, g_ref, w_ref, cos_ref, sa_ref, sb_ref, oh_ref,
                   u_ref, q_ref, kc_ref, vc_ref, ks_ref, vs_ref, kw_ref, vw_ref,
                   ksa_ref, vsb_ref, kwb_ref, vwb_ref, gate_ref):
    cc = u_ref.shape[1]
    xb = _rms(x_ref[...], g_ref[...]).astype(_MXU_DTYPE)
    cos, sa, sb = cos_ref[...], sa_ref[...], sb_ref[...]

    def seg(lo, hi):
        return _dot(xb, w_ref[:, lo:hi])

    def rope(v):
        return v * cos + pltpu.roll(v, LANES - ROT_DIM // 2, 1) * sa + pltpu.roll(v, ROT_DIM // 2, 1) * sb

    z = seg(0, 2 * cc)
    u_ref[...] = z[:, :cc] * jax.nn.sigmoid(z[:, cc:])
    c0 = 2 * cc
    zq = seg(c0, c0 + N_HEADS * HEAD_DIM)
    for c in range(GQA_R):
        q_ref[:, c * LANES:(c + 1) * LANES] = rope(zq[:, c * LANES:(c + 1) * LANES]).astype(q_ref.dtype)
    c1 = c0 + N_HEADS * HEAD_DIM
    zkv = seg(c1, c1 + 6 * KV_W)
    kc = rope(zkv[:, 0 * KV_W:1 * KV_W])
    vc = zkv[:, 1 * KV_W:2 * KV_W]
    ks = rope(zkv[:, 2 * KV_W:3 * KV_W])
    vs = zkv[:, 3 * KV_W:4 * KV_W]
    kw = rope(zkv[:, 4 * KV_W:5 * KV_W])
    vw = zkv[:, 5 * KV_W:6 * KV_W]
    kc_ref[...] = kc
    vc_ref[...] = vc
    ks_ref[...] = ks
    vs_ref[...] = vs
    kw_ref[...] = kw
    vw_ref[...] = vw
    ksa_ref[:, :KV_W] = ks.astype(ksa_ref.dtype)
    ksa_ref[:, KV_W:] = oh_ref[...]
    vsb_ref[...] = vs.astype(vsb_ref.dtype)
    kwb_ref[...] = kw.astype(kwb_ref.dtype)
    vwb_ref[...] = vw.astype(vwb_ref.dtype)
    c2 = c1 + 6 * KV_W
    gate_ref[...] = jax.nn.sigmoid(seg(c2, c2 + LANES))


def _rope_tables(pos):
    half = ROT_DIM // 2
    inv = jnp.power(jnp.float32(ROPE_THETA), -jnp.arange(half, dtype=jnp.float32) * 2.0 / ROT_DIM)
    ang = pos.astype(jnp.float32)[:, None] * inv[None, :]
    cos, sin = jnp.cos(ang), jnp.sin(ang)
    n = pos.shape[0]
    ones = jnp.ones((n, HEAD_DIM - ROT_DIM), F32)
    zeros8 = jnp.zeros((n, half), F32)
    zeros_rest = jnp.zeros((n, HEAD_DIM - ROT_DIM), F32)
    cos_h = jnp.concatenate([cos, cos, ones], 1)
    sa_h = jnp.concatenate([-sin, zeros8, zeros_rest], 1)
    sb_h = jnp.concatenate([zeros8, sin, zeros_rest], 1)
    two = lambda a: jnp.concatenate([a, a], 1)
    return two(cos_h), two(sa_h), two(sb_h)


def _block_onehot(pos):
    j = jnp.arange(LANES)[None, :]
    return ((pos[:, None] // SEL_BLOCK) == j).astype(_MXU_DTYPE)


def _inproj(x2d, pos, g_pre, w_cat, cc):
    n, d = x2d.shape
    period = pos.shape[0]
    tile = 512 if (n % 512 == 0 and period % 512 == 0) else period
    assert n % tile == 0 and period % tile == 0
    tpp = period // tile
    cos, sa, sb = _rope_tables(pos)
    oh = _block_onehot(pos)
    tok = lambda w: pl.BlockSpec((tile, w), lambda i: (i, 0))
    tab = lambda w: pl.BlockSpec((tile, w), lambda i: (i % tpp, 0))
    f32o = lambda w: jax.ShapeDtypeStruct((n, w), F32)
    b16o = lambda w: jax.ShapeDtypeStruct((n, w), _MXU_DTYPE)
    out_shape = (f32o(cc), b16o(N_HEADS * HEAD_DIM)) + (f32o(KV_W),) * 6 + (
        b16o(2 * KV_W), b16o(KV_W), b16o(KV_W), b16o(KV_W), f32o(LANES))
    out_specs = (tok(cc), tok(N_HEADS * HEAD_DIM)) + (tok(KV_W),) * 6 + (
        tok(2 * KV_W), tok(KV_W), tok(KV_W), tok(KV_W), tok(LANES))
    return pl.pallas_call(
        _inproj_kernel,
        grid=(n // tile,),
        in_specs=[tok(d), _const_spec((1, d)), _const_spec(w_cat.shape), tab(LANES), tab(LANES), tab(LANES),
                  tab(LANES)],
        out_specs=out_specs,
        out_shape=out_shape,
        compiler_params=_params(("parallel",)),
        name="inproj",
    )(x2d, g_pre.reshape(1, d), w_cat, cos, sa, sb, oh)


def _ln_silu(y, g, b):
    yc = y - jnp.mean(y, -1, keepdims=True)
    yn = yc * lax.rsqrt(jnp.mean(yc * yc, -1, keepdims=True) + EPS) * g + b
    return yn * jax.nn.sigmoid(yn)


def _conv_prompt_kernel(u_ref, halo_ref, w_ref, b_ref, g_ref, bb_ref, o_ref, ext_ref, *, rows):
    i = pl.program_id(1)
    tc = u_ref.shape[1]
    ext_ref[0:HALO, :] = jnp.where(i == 0, 0.0, halo_ref[0])
    ext_ref[HALO:, :] = u_ref[0]
    lead = HALO - (CONV_K - 1)

    for c in range(tc // rows):
        base = c * rows
        acc = jnp.zeros((rows, u_ref.shape[2]), F32)
        for r in range(8):
            taps = [k for k in range(CONV_K) if (lead + k) % 8 == r]
            span = max((lead + k) // 8 for k in taps) * 8 + rows
            win = ext_ref[base + r:base + r + span, :]
            for k in taps:
                a = (lead + k) // 8 * 8
                acc = acc + win[a:a + rows] * w_ref[k:k + 1, :]
        y = _ln_silu(acc + b_ref[...], g_ref[...], bb_ref[...])
        o_ref[0, base:base + rows, :] = y.astype(o_ref.dtype)


def _conv_prompt(u, conv_w, conv_b, ln_g, ln_b):
    bsz, t, cc = u.shape
    tc = 512 if t % 512 == 0 else t
    rows = 64
    assert t % tc == 0 and tc % rows == 0 and tc % HALO == 0
    hb = tc // HALO
    vec = lambda a: a.reshape(1, cc)
    return pl.pallas_call(
        functools.partial(_conv_prompt_kernel, rows=rows),
        grid=(bsz, t // tc),
        in_specs=[pl.BlockSpec((1, tc, cc), lambda b, i: (b, i, 0)),
                  pl.BlockSpec((1, HALO, cc), lambda b, i: (b, jnp.maximum(i * hb - 1, 0), 0)),
                  _const_spec((CONV_K, cc)), _const_spec((1, cc)), _const_spec((1, cc)), _const_spec((1, cc))],
        out_specs=pl.BlockSpec((1, tc, cc), lambda b, i: (b, i, 0)),
        out_shape=jax.ShapeDtypeStruct((bsz, t, cc), _MXU_DTYPE),
        scratch_shapes=[pltpu.VMEM((tc + HALO, cc), F32)],
        compiler_params=_params(("parallel", "parallel")),
        name="conv_prompt",
    )(u, u, conv_w, vec(conv_b), vec(ln_g), vec(ln_b))


def _conv_sample_kernel(cache_ref, u_ref, w_ref, b_ref, g_ref, bb_ref, o_ref):
    nb = cache_ref.shape[0]
    w_hist = w_ref[0:CONV_K - 1, :]
    rows = [jnp.sum(cache_ref[i] * w_hist, axis=0, keepdims=True) for i in range(nb)]
    y = jnp.concatenate(rows, 0) + u_ref[...] * w_ref[CONV_K - 1:CONV_K, :] + b_ref[...]
    o_ref[...] = _ln_silu(y, g_ref[...], bb_ref[...])


def _conv_sample(cache_conv, u, conv_w, conv_b, ln_g, ln_b):
    nb, hist, cc = cache_conv.shape
    blk = 8
    assert nb % blk == 0 and hist == CONV_K - 1
    vec = lambda a: a.reshape(1, cc)
    return pl.pallas_call(
        _conv_sample_kernel,
        grid=(nb // blk,),
        in_specs=[pl.BlockSpec((blk, hist, cc), lambda i: (i, 0, 0)), pl.BlockSpec((blk, cc), lambda i: (i, 0)),
                  _const_spec((CONV_K, cc)), _const_spec((1, cc)), _const_spec((1, cc)), _const_spec((1, cc))],
        out_specs=pl.BlockSpec((blk, cc), lambda i: (i, 0)),
        out_shape=jax.ShapeDtypeStruct((nb, cc), F32),
        compiler_params=_params(("parallel",)),
        name="conv_sample",
    )(cache_conv, u, conv_w, vec(conv_b), vec(ln_g), vec(ln_b))


def _compress_rows(rows_ref, wbig_ref, bias_ref, w2_ref):
    nch = rows_ref.shape[0] // CMP_STRIDE
    acc = None
    for jj in range(CMP_STRIDE // 2):
        a = rows_ref[pl.ds(2 * jj, nch, stride=CMP_STRIDE), :]
        b = rows_ref[pl.ds(2 * jj + 1, nch, stride=CMP_STRIDE), :]
        d = _dot(jnp.concatenate([a, b], axis=1).astype(_MXU_DTYPE), wbig_ref[jj])
        acc = d if acc is None else acc + d
    hw = N_KV * CMP_HIDDEN
    first, second = acc[:, :hw], acc[:, hw:]
    second_next = pltpu.roll(second, nch - 1, 0)
    h = jax.nn.gelu(first + second_next + bias_ref[...], approximate=True)
    out = _dot(h.astype(_MXU_DTYPE), w2_ref[...])
    row = lax.broadcasted_iota(jnp.int32, out.shape, 0)
    return jnp.where(row < nch - 1, out, 0.0)


def _compress_weights(w1, b1, w2, pe):
    eye = jnp.eye(N_KV, dtype=F32)
    w1s = w1.reshape(2, CMP_STRIDE // 2, 2, HEAD_DIM, CMP_HIDDEN)
    wbig = jnp.einsum('sjldh,ge->jlgdseh', w1s, eye)
    wbig = wbig.reshape(CMP_STRIDE // 2, 2 * KV_W, 2 * N_KV * CMP_HIDDEN).astype(_MXU_DTYPE)
    bias = b1 + jnp.einsum('jd,jdh->h', pe, w1, precision=lax.Precision.HIGHEST)
    bias2 = jnp.tile(bias, N_KV).reshape(1, N_KV * CMP_HIDDEN)
    w2big = jnp.einsum('hd,ge->ghed', w2, eye).reshape(N_KV * CMP_HIDDEN, KV_W).astype(_MXU_DTYPE)
    return wbig, bias2, w2big


def _cmp_prompt_kernel(rk_ref, rv_ref, wk_ref, bk_ref, w2k_ref, wv_ref, bv_ref, w2v_ref, kc_ref, vc_ref):
    kc_ref[0] = _compress_rows(rk_ref.at[0], wk_ref, bk_ref, w2k_ref).astype(kc_ref.dtype)
    vc_ref[0] = _compress_rows(rv_ref.at[0], wv_ref, bv_ref, w2v_ref).astype(vc_ref.dtype)


def _cmp_prompt(rows_k, rows_v, wk, wv):
    bsz, t, _ = rows_k.shape
    nch = t // CMP_STRIDE
    assert t % CMP_STRIDE == 0
    wspecs = [_const_spec(a.shape) for a in wk]
    row_spec = pl.BlockSpec((1, t, KV_W), lambda b: (b, 0, 0))
    out_spec = pl.BlockSpec((1, nch, KV_W), lambda b: (b, 0, 0))
    out = jax.ShapeDtypeStruct((bsz, nch, KV_W), _MXU_DTYPE)
    return pl.pallas_call(
        _cmp_prompt_kernel,
        grid=(bsz,),
        in_specs=[row_spec, row_spec] + wspecs + wspecs,
        out_specs=(out_spec, out_spec),
        out_shape=(out, out),
        compiler_params=_params(("parallel",)),
        name="cmp_prompt",
    )(rows_k, rows_v, *wk, *wv)


PAGE_ROWS = PAGE_SIZE * N_KV
BLOCK_ROWS = SEL_BLOCK * N_KV


def _paged_compress_weights(w1, b1, w2, pe):
    w1j = w1.reshape(2, CMP_STRIDE, HEAD_DIM, CMP_HIDDEN).transpose(1, 2, 0, 3)
    w1j = w1j.reshape(CMP_STRIDE, HEAD_DIM, 2 * CMP_HIDDEN).astype(_MXU_DTYPE)
    bias = (b1 + jnp.einsum('jd,jdh->h', pe, w1, precision=lax.Precision.HIGHEST)).reshape(1, CMP_HIDDEN)
    eye = jnp.eye(N_KV, dtype=F32)
    w2g = jnp.einsum('hd,ge->ghed', w2, eye).reshape(N_KV, CMP_HIDDEN, KV_W).astype(_MXU_DTYPE)
    return w1j, bias, w2g


def _cmp_sample_kernel(pt_ref, pk_ref, pv_ref, w1_ref, b_ref, w2_ref, out_ref, buf, sem, *, n_pages):
    b = pl.program_id(0)
    s = pl.program_id(1)

    def copies(pool_ref, seq, slot, wait):
        for p in range(n_pages):
            row0 = 0 if wait else pl.multiple_of(pt_ref[seq, p] * PAGE_ROWS, PAGE_ROWS)
            cp = pltpu.make_async_copy(pool_ref.at[pl.ds(row0, PAGE_ROWS), :],
                                       buf.at[slot, pl.ds(p * PAGE_ROWS, PAGE_ROWS), :], sem.at[slot])
            if wait:
                cp.wait()
            else:
                cp.start()

    @pl.when((b == 0) & (s == 0))
    def _():
        copies(pk_ref, 0, 0, False)

    @pl.when(s == 0)
    def _():
        copies(pv_ref, b, 1, False)

    @pl.when((s == 1) & (b + 1 < pl.num_programs(0)))
    def _():
        copies(pk_ref, b + 1, 0, False)

    copies(pk_ref, b, s, True)

    rows_ref = buf.at[s]
    nch = rows_ref.shape[0] // (CMP_STRIDE * N_KV)
    acc = [None] * N_KV
    for j in range(CMP_STRIDE):
        for g in range(N_KV):
            x = rows_ref[pl.ds(N_KV * j + g, nch, stride=CMP_STRIDE * N_KV), :]
            d = _dot(x.astype(_MXU_DTYPE), w1_ref[0, j])
            acc[g] = d if acc[g] is None else acc[g] + d
    out = None
    for g in range(N_KV):
        first, second = acc[g][:, :CMP_HIDDEN], acc[g][:, CMP_HIDDEN:]
        h = jax.nn.gelu(first + pltpu.roll(second, nch - 1, 0) + b_ref[0], approximate=True)
        d = _dot(h.astype(_MXU_DTYPE), w2_ref[0, g])
        out = d if out is None else out + d
    row = lax.broadcasted_iota(jnp.int32, out.shape, 0)
    out_ref[0, 0] = jnp.where(row < nch - 1, out, 0.0).astype(out_ref.dtype)


def _cmp_sample(page_table, pool_k, pool_v, wk, wv):
    nb, n_pages = page_table.shape
    past = n_pages * PAGE_SIZE
    nch = past // CMP_STRIDE
    w1, bias, w2 = (jnp.stack([a, c]) for a, c in zip(wk, wv))
    sel = lambda a: pl.BlockSpec((1,) + a.shape[1:], lambda b, s, pt, n=a.ndim: (s,) + (0,) * (n - 1))
    any_spec = pl.BlockSpec(memory_space=pl.ANY)
    return pl.pallas_call(
        functools.partial(_cmp_sample_kernel, n_pages=n_pages),
        grid_spec=pltpu.PrefetchScalarGridSpec(
            num_scalar_prefetch=1,
            grid=(nb, 2),
            in_specs=[any_spec, any_spec, sel(w1), sel(bias), sel(w2)],
            out_specs=pl.BlockSpec((1, 1, nch, KV_W), lambda b, s, pt: (s, b, 0, 0)),
            scratch_shapes=[pltpu.VMEM((2, past * N_KV, HEAD_DIM), F32), pltpu.SemaphoreType.DMA((2,))]),
        out_shape=jax.ShapeDtypeStruct((2, nb, nch, KV_W), _MXU_DTYPE),
        compiler_params=_params(("arbitrary", "arbitrary")),
        name="cmp_sample",
    )(page_table, pool_k, pool_v, w1, bias, w2)


def _gate_expand_matrix():
    e = np.zeros((2 * LANES, 3 * GQA_R * LANES), np.float32)
    for br in range(3):
        for g in range(N_KV):
            for r in range(GQA_R):
                c = br * N_HEADS + g * GQA_R + r
                lo = (br * GQA_R + r) * LANES + g * HEAD_DIM
                e[c, lo:lo + HEAD_DIM] = 1.0
                e[LANES + c, lo:lo + HEAD_DIM] = 1.0
    return jnp.asarray(e, dtype=_MXU_DTYPE)


def _expand_gates(gt, e_ref):
    hi = gt.astype(_MXU_DTYPE)
    lo = (gt - hi.astype(F32)).astype(_MXU_DTYPE)
    return _dot(jnp.concatenate([hi, lo], axis=1), e_ref[...])


def _overlap_t(n_cmp_pad, n_blk_pad, n_cmp, n_blk):
    i = np.arange(n_cmp_pad)[None, :]
    j = np.arange(n_blk_pad)[:, None]
    lo = np.maximum(i * CMP_STRIDE, j * SEL_BLOCK)
    hi = np.minimum(i * CMP_STRIDE + CMP_LEN, (j + 1) * SEL_BLOCK)
    ov = (np.maximum(hi - lo, 0) // CMP_STRIDE).astype(np.float32)
    ov = ov * (i < n_cmp) * (j < n_blk)
    return ov


def _nsa_prompt_kernel(q_ref, gate_ref, kc_ref, vc_ref, ksa_ref, vs_ref, kw_ref, vw_ref, ovt_ref, e_ref, o_ref,
                       m_ref, l_ref, acc_ref, *, seq_len):
    tq = q_ref.shape[1]
    rows = GQA_R * tq
    start = pl.program_id(1) * tq
    qt = q_ref[0]
    lane = lax.broadcasted_iota(jnp.int32, (tq, LANES), 1)
    gx = _expand_gates(gate_ref[0], e_ref)
    kc = kc_ref[0]
    vc = vc_ref[0]
    n_cmp = kc.shape[0]
    n_sel_rows = SEL_BLOCK

    def tok(shape):
        return start + (lax.broadcasted_iota(jnp.int32, shape, 0) & (tq - 1))

    def rep(v, n):
        return jnp.concatenate([v] * n, axis=1) if n > 1 else v

    wk = WINDOW + tq
    ws = pl.multiple_of(jnp.clip(start - WINDOW, 0, seq_len - wk), tq)
    n_full = start // SEL_CHUNK

    branch = [[None] * N_KV for _ in range(3)]
    for g in range(N_KV):
        mine = (lane < HEAD_DIM) if g == 0 else (lane >= HEAD_DIM)
        lhs = jnp.concatenate(
            [jnp.where(mine, qt[:, r * LANES:(r + 1) * LANES], jnp.zeros_like(qt[:, :LANES])) for r in range(GQA_R)],
            axis=0)

        s = _dot_nt(lhs, kc)
        vis = lax.broadcasted_iota(jnp.int32, s.shape, 1) * CMP_STRIDE + (CMP_LEN - 1) <= tok(s.shape)
        sm = jnp.where(vis, s, NEG)
        e = jnp.where(vis, jnp.exp2(sm - jnp.max(sm, -1, keepdims=True)), 0.0)
        p = e * (1.0 / jnp.maximum(jnp.sum(e, -1, keepdims=True), 1e-30))
        branch[0][g] = _dot(p.astype(_MXU_DTYPE), vc)
        psum = (p[0:tq] + p[tq:2 * tq]) + p[2 * tq:3 * tq] + p[3 * tq:4 * tq]
        imp = _dot_nt(ovt_ref[...], psum.astype(_MXU_DTYPE))[:n_sel_rows]

        j = lax.broadcasted_iota(jnp.int32, imp.shape, 0)
        cur = (start + lax.broadcasted_iota(jnp.int32, imp.shape, 1)) // SEL_BLOCK
        future = j > cur
        forced = (j == 0) | (j == cur) | (j == cur - 1)
        score = jnp.where(future, -jnp.inf, jnp.where(forced, jnp.inf, imp))
        slabs = [score[8 * v:8 * v + 8] for v in range(n_sel_rows // 8)]
        ranks = [jnp.zeros((8, tq), jnp.int32) for _ in slabs]
        sub = lax.broadcasted_iota(jnp.int32, (8, tq), 0)
        for i in range(n_sel_rows):
            si = score[i:i + 1, :]
            for v, sl in enumerate(slabs):
                if v < i // 8:
                    inc = jnp.where(si > sl, 1, 0)
                elif v > i // 8:
                    inc = jnp.where(si >= sl, 1, 0)
                else:
                    inc = jnp.where(sub > i % 8, jnp.where(si >= sl, 1, 0), jnp.where(si > sl, 1, 0))
                ranks[v] = ranks[v] + inc
        rank = jnp.concatenate(ranks, axis=0)
        bias_t = jnp.where(future, SEL_OFF, jnp.where(rank < N_SEL, 0.0, SEL_OFF))
        bias_t = jnp.concatenate([bias_t, jnp.zeros((LANES - n_sel_rows, tq), F32)], axis=0)
        bias = jnp.transpose(bias_t).astype(_MXU_DTYPE)
        lhs2 = jnp.concatenate([lhs, jnp.concatenate([bias] * GQA_R, axis=0)], axis=1)

        m_ref[...] = jnp.full(m_ref.shape, NEG, F32)
        l_ref[...] = jnp.zeros(l_ref.shape, F32)
        acc_ref[...] = jnp.zeros(acc_ref.shape, F32)

        def sel_step(c, carry, lhs2=lhs2, causal=False):
            k0 = pl.multiple_of(c * SEL_CHUNK, SEL_CHUNK)
            s = _dot_nt(lhs2, ksa_ref[0, pl.ds(k0, SEL_CHUNK), :])
            if causal:
                kpos = k0 + lax.broadcasted_iota(jnp.int32, s.shape, 1)
                s = jnp.where(kpos <= tok(s.shape), s, NEG)
            m_prev = m_ref[...]
            m_new = jnp.maximum(m_prev, jnp.max(s, -1, keepdims=True))
            alpha = jnp.exp2(m_prev - m_new)
            pe = jnp.exp2(s - rep(m_new, SEL_CHUNK // LANES))
            l_ref[...] = alpha * l_ref[...] + jnp.sum(pe, -1, keepdims=True)
            acc_ref[...] = alpha * acc_ref[...] + _dot(pe.astype(_MXU_DTYPE), vs_ref[0, pl.ds(k0, SEL_CHUNK), :])
            m_ref[...] = m_new
            return carry

        lax.fori_loop(0, n_full, sel_step, 0)
        sel_step(n_full, 0, causal=True)
        branch[1][g] = acc_ref[...] / l_ref[...]

        s = _dot_nt(lhs, kw_ref[0, pl.ds(ws, wk), :])
        kpos = ws + lax.broadcasted_iota(jnp.int32, s.shape, 1)
        t = tok(s.shape)
        s = jnp.where(kpos <= t, jnp.where(kpos >= t - WINDOW, s, NEG), NEG)
        e = jnp.exp2(s - jnp.max(s, -1, keepdims=True))
        p = e * (1.0 / jnp.sum(e, -1, keepdims=True))
        branch[2][g] = _dot(p.astype(_MXU_DTYPE), vw_ref[0, pl.ds(ws, wk), :])

    first_half = lane < HEAD_DIM
    for r in range(GQA_R):
        out = jnp.zeros((tq, LANES), F32)
        for br in range(3):
            both = jnp.where(first_half, branch[br][0][r * tq:(r + 1) * tq], branch[br][1][r * tq:(r + 1) * tq])
            col = (br * GQA_R + r) * LANES
            out = out + gx[:, col:col + LANES] * both
        o_ref[0, :, r * LANES:(r + 1) * LANES] = out.astype(o_ref.dtype)


def _nsa_prompt(q, gates, kc, vc, ksa, vsb, kwb, vwb):
    bsz, t, qw = q.shape
    tq = 128
    n_cmp = kc.shape[1]
    assert t % SEL_CHUNK == 0 and t >= WINDOW + tq and t // SEL_BLOCK <= SEL_BLOCK and qw == GQA_R * LANES
    ovt = jnp.asarray(_overlap_t(n_cmp, LANES, n_cmp - 1, t // SEL_BLOCK), dtype=_MXU_DTYPE)
    e = _gate_expand_matrix()
    tile = lambda w: pl.BlockSpec((1, tq, w), lambda b, i: (b, i, 0))
    full = lambda n, w: pl.BlockSpec((1, n, w), lambda b, i: (b, 0, 0))
    rows = GQA_R * tq
    return pl.pallas_call(
        functools.partial(_nsa_prompt_kernel, seq_len=t),
        grid=(bsz, t // tq),
        in_specs=[tile(qw), tile(LANES), full(n_cmp, KV_W), full(n_cmp, KV_W), full(t, 2 * KV_W), full(t, KV_W),
                  full(t, KV_W), full(t, KV_W), _const_spec(ovt.shape), _const_spec(e.shape)],
        out_specs=tile(qw),
        out_shape=jax.ShapeDtypeStruct((bsz, t, qw), _MXU_DTYPE),
        scratch_shapes=[pltpu.VMEM((rows, LANES), F32)] * 3,
        compiler_params=_params(("parallel", "arbitrary")),
        name="nsa_prompt",
    )(q, gates, kc, vc, ksa, vsb, kwb, vwb, ovt, e)


SEQ_BLK = 8


def _sample_select_kernel(q_ref, kc_ref, vc_ref, ov_ref, oc_ref, idx_ref, *, n_vis, n_past_blk):
    psums = []
    for bi in range(SEQ_BLK):
        s = _dot_nt(q_ref[bi].astype(_MXU_DTYPE), kc_ref[0, bi])
        vis = lax.broadcasted_iota(jnp.int32, s.shape, 1) < n_vis
        sm = jnp.where(vis, s, NEG)
        e = jnp.where(vis, jnp.exp2(sm - jnp.max(sm, -1, keepdims=True)), 0.0)
        p = e * (1.0 / jnp.maximum(jnp.sum(e, -1, keepdims=True), 1e-30))
        oc = _dot(p.astype(_MXU_DTYPE), vc_ref[0, bi])
        row = lax.broadcasted_iota(jnp.int32, p.shape, 0)
        top = lax.broadcasted_iota(jnp.int32, (N_HEADS, HEAD_DIM), 0) < GQA_R
        oc_ref[bi] = jnp.where(top, oc[:, :HEAD_DIM], oc[:, HEAD_DIM:])
        psums.append(jnp.sum(jnp.where(row < GQA_R, p, 0.0), axis=0, keepdims=True))
        psums.append(jnp.sum(jnp.where(row >= GQA_R, p, 0.0), axis=0, keepdims=True))
    psum = jnp.concatenate(psums, axis=0)
    imp = _dot(psum.astype(_MXU_DTYPE), ov_ref[...])
    j = lax.broadcasted_iota(jnp.int32, imp.shape, 1)
    score = jnp.where(j >= n_past_blk, -jnp.inf,
                      jnp.where(j == 0, jnp.inf, jnp.where(j == n_past_blk - 1, jnp.inf, imp)))
    picks = jnp.zeros(imp.shape, jnp.int32)
    for it in range(N_SEL - 1):
        best = jnp.max(score, -1, keepdims=True)
        pick = jnp.min(jnp.where(score == best, j, LANES), -1, keepdims=True)
        picks = jnp.where(j == it, pick, picks)
        score = jnp.where(j == pick, -jnp.inf, score)
    idx_ref[...] = picks


def _sample_select(q_pad, kvc, n_past_blk):
    nb = q_pad.shape[0]
    n_cmp_pad = kvc.shape[2]
    assert n_past_blk <= LANES and n_past_blk >= N_SEL - 1 and nb % SEQ_BLK == 0
    ov = jnp.asarray(_overlap_t(n_cmp_pad, LANES, n_cmp_pad - 1, n_past_blk).T, dtype=_MXU_DTYPE)
    return pl.pallas_call(
        functools.partial(_sample_select_kernel, n_vis=n_cmp_pad - 1, n_past_blk=n_past_blk),
        grid=(nb // SEQ_BLK,),
        in_specs=[pl.BlockSpec((SEQ_BLK, N_HEADS, LANES), lambda i: (i, 0, 0)),
                  pl.BlockSpec((1, SEQ_BLK, n_cmp_pad, KV_W), lambda i: (0, i, 0, 0)),
                  pl.BlockSpec((1, SEQ_BLK, n_cmp_pad, KV_W), lambda i: (1, i, 0, 0)),
                  _const_spec(ov.shape)],
        out_specs=(pl.BlockSpec((SEQ_BLK, N_HEADS, HEAD_DIM), lambda i: (i, 0, 0)),
                   pl.BlockSpec((SEQ_BLK * N_KV, LANES), lambda i: (i, 0))),
        out_shape=(jax.ShapeDtypeStruct((nb, N_HEADS, HEAD_DIM), F32),
                   jax.ShapeDtypeStruct((nb * N_KV, LANES), jnp.int32)),
        compiler_params=_params(("parallel",)),
        name="sample_select",
    )(q_pad, kvc, kvc, ov)


def _block_copies(idx_ref, pt_ref, pool_ref, buf_ref, sem, seq, slot, wait):
    per_page = PAGE_SIZE // SEL_BLOCK
    for g in range(N_KV):
        for i in range(N_SEL - 1):
            if wait:
                row0 = 0
            else:
                blk = idx_ref[seq * N_KV + g, i]
                row0 = pl.multiple_of(pt_ref[seq, blk // per_page] * PAGE_ROWS + (blk % per_page) * BLOCK_ROWS,
                                      BLOCK_ROWS)
            cp = pltpu.make_async_copy(pool_ref.at[pl.ds(row0, BLOCK_ROWS), :],
                                       buf_ref.at[slot, g, pl.ds(i * BLOCK_ROWS, BLOCK_ROWS), :], sem)
            if wait:
                cp.wait()
            else:
                cp.start()


def _attend_with_new(q, keys, vals, k_new, v_new):
    top = lax.broadcasted_iota(jnp.int32, (N_HEADS, 1), 0) < GQA_R
    per_group = isinstance(keys, (list, tuple))
    cast = lambda a: a.astype(_MXU_DTYPE)
    if per_group:
        s = jnp.where(top, _dot_nt(q, cast(keys[0])), _dot_nt(q, cast(keys[1])))
    else:
        s = _dot_nt(q, cast(keys))
    mine = (lax.broadcasted_iota(jnp.int32, s.shape, 1) & 1) == jnp.where(top, 0, 1)
    s = jnp.where(mine, s, NEG)
    kn = jnp.where(top, k_new[0:1], k_new[1:2]).astype(_MXU_DTYPE).astype(F32)
    vn = jnp.where(top, v_new[0:1], v_new[1:2]).astype(_MXU_DTYPE).astype(F32)
    s_new = jnp.sum(q.astype(F32) * kn, -1, keepdims=True)
    m = jnp.maximum(jnp.max(s, -1, keepdims=True), s_new)
    e = jnp.exp2(s - m)
    e_new = jnp.exp2(s_new - m)
    eb = e.astype(_MXU_DTYPE)
    if per_group:
        pv = jnp.where(top, _dot(eb, cast(vals[0])), _dot(eb, cast(vals[1])))
    else:
        pv = _dot(eb, cast(vals))
    return (pv + e_new * vn) / (jnp.sum(e, -1, keepdims=True) + e_new)


def _sample_attend_kernel(idx_ref, pt_ref, q_ref, gate_ref, oc_ref, ksn_ref, vsn_ref, kwn_ref, vwn_ref,
                          kwin_ref, vwin_ref, pks_ref, pvs_ref, o_ref, kbuf, vbuf, sem):
    b = pl.program_id(0)
    slot = b % 2

    def gather(seq, sl, wait):
        _block_copies(idx_ref, pt_ref, pks_ref, kbuf, sem.at[0, sl], seq, sl, wait)
        _block_copies(idx_ref, pt_ref, pvs_ref, vbuf, sem.at[1, sl], seq, sl, wait)

    @pl.when(b == 0)
    def _():
        gather(0, 0, False)

    @pl.when(b + 1 < pl.num_programs(0))
    def _():
        gather(b + 1, 1 - slot, False)

    q = q_ref[0].astype(_MXU_DTYPE)
    o_w = _attend_with_new(q, kwin_ref[0], vwin_ref[0], kwn_ref[0], vwn_ref[0])
    gather(b, slot, True)
    o_s = _attend_with_new(q, [kbuf[slot, g] for g in range(N_KV)], [vbuf[slot, g] for g in range(N_KV)],
                           ksn_ref[0], vsn_ref[0])
    gates = jnp.broadcast_to(gate_ref[0], (N_HEADS, LANES))
    col = lax.broadcasted_iota(jnp.int32, gates.shape, 1) - lax.broadcasted_iota(jnp.int32, gates.shape, 0)
    out = jnp.zeros((N_HEADS, HEAD_DIM), F32)
    for br, o_br in enumerate((oc_ref[0], o_s, o_w)):
        out = out + jnp.sum(jnp.where(col == br * N_HEADS, gates, 0.0), -1, keepdims=True) * o_br
    o_ref[0] = out


def _sample_attend(idx, page_table, q8, gates, o_c, ks_new, vs_new, kw_new, vw_new, buf_k_win, buf_v_win,
                   pool_k_sel, pool_v_sel):
    nb = q8.shape[0]
    wrows = buf_k_win.shape[1]
    n_rows = (N_SEL - 1) * BLOCK_ROWS
    per = lambda n, w: pl.BlockSpec((1, n, w), lambda b, *_: (b, 0, 0))
    any_spec = pl.BlockSpec(memory_space=pl.ANY)
    new = lambda a: a.reshape(nb, N_KV, HEAD_DIM)
    return pl.pallas_call(
        _sample_attend_kernel,
        grid_spec=pltpu.PrefetchScalarGridSpec(
            num_scalar_prefetch=2,
            grid=(nb,),
            in_specs=[per(N_HEADS, HEAD_DIM), per(1, LANES), per(N_HEADS, HEAD_DIM)] + [per(N_KV, HEAD_DIM)] * 4 + [
                per(wrows, HEAD_DIM), per(wrows, HEAD_DIM), any_spec, any_spec],
            out_specs=per(N_HEADS, HEAD_DIM),
            scratch_shapes=[pltpu.VMEM((2, N_KV, n_rows, HEAD_DIM), F32), pltpu.VMEM((2, N_KV, n_rows, HEAD_DIM), F32),
                            pltpu.SemaphoreType.DMA((2, 2))]),
        out_shape=jax.ShapeDtypeStruct((nb, N_HEADS, HEAD_DIM), F32),
        compiler_params=_params(("arbitrary",)),
        name="sample_attend",
    )(idx, page_table, q8, gates.reshape(nb, 1, LANES), o_c, new(ks_new), new(vs_new), new(kw_new), new(vw_new),
      buf_k_win, buf_v_win, pool_k_sel, pool_v_sel)


def _layer_out_kernel(x_ref, cy_ref, o_ref, woc_ref, woo_ref, gpost_ref, gpre2_ref, wup_ref, wdn_ref, gpost2_ref,
                      y_ref, *, ff_chunk):
    mix = _dot(cy_ref[...], woc_ref[...]) + _dot(o_ref[...], woo_ref[...])
    h = x_ref[...] + _rms(mix, gpost_ref[...])
    hn = _rms(h, gpre2_ref[...]).astype(_MXU_DTYPE)
    f = None
    for c in range(wup_ref.shape[1] // ff_chunk):
        a = jnp.maximum(_dot(hn, wup_ref[:, c * ff_chunk:(c + 1) * ff_chunk]), 0.0)
        d = _dot((a * a).astype(_MXU_DTYPE), wdn_ref[c * ff_chunk:(c + 1) * ff_chunk, :])
        f = d if f is None else f + d
    y_ref[...] = h + _rms(f, gpost2_ref[...])


def _layer_out(x2d, cy, o, w_out_c, w_out_o, g_post, g_pre2, w_up, w_down, g_post2):
    n, d = x2d.shape
    tile = 512 if n % 512 == 0 else n
    tok = lambda w: pl.BlockSpec((tile, w), lambda i: (i, 0))
    vec = lambda a: a.reshape(1, d)
    return pl.pallas_call(
        functools.partial(_layer_out_kernel, ff_chunk=1024),
        grid=(n // tile,),
        in_specs=[tok(d), tok(cy.shape[1]), tok(o.shape[1]), _const_spec(w_out_c.shape), _const_spec(w_out_o.shape),
                  _const_spec((1, d)), _const_spec((1, d)), _const_spec(w_up.shape), _const_spec(w_down.shape),
                  _const_spec((1, d))],
        out_specs=tok(d),
        out_shape=jax.ShapeDtypeStruct((n, d), F32),
        compiler_params=_params(("parallel",)),
        name="layer_out",
    )(x2d, cy, o, w_out_c, w_out_o, vec(g_post), vec(g_pre2), w_up, w_down, vec(g_post2))


def _prep_w_in(w_in, cc):
    d = w_in.shape[0]
    c0 = 2 * cc
    c1 = c0 + N_HEADS * HEAD_DIM
    c2 = c1 + 6 * KV_W
    wq = w_in[:, c0:c1].reshape(d, N_KV, GQA_R, HEAD_DIM).transpose(0, 2, 1, 3).reshape(d, N_HEADS * HEAD_DIM)
    wq = wq * (HEAD_DIM ** -0.5 * LOG2E)
    wg = jnp.pad(w_in[:, c2:], ((0, 0), (0, LANES - N_GATE)))
    return jnp.concatenate([w_in[:, :c0], wq, w_in[:, c1:c2], wg], axis=1).astype(_MXU_DTYPE)


def _prep_w_out(w_out, cc):
    d = w_out.shape[1]
    wo = w_out[cc:].reshape(N_KV, GQA_R, HEAD_DIM, d).transpose(1, 0, 2, 3).reshape(N_HEADS * HEAD_DIM, d)
    return w_out[:cc].astype(_MXU_DTYPE), wo.astype(_MXU_DTYPE)


def kernel(x_prompt, x_sample, cache_k_cmp, cache_v_cmp, cache_k_sel, cache_v_sel, cache_k_win, cache_v_win, cache_conv, page_table, g_pre_mix, w_in, conv_w, conv_b, conv_ln_g, conv_ln_b, cmp_k_w1, cmp_k_b1, cmp_k_w2, cmp_k_pe, cmp_v_w1, cmp_v_b1, cmp_v_w2, cmp_v_pe, w_out, g_post_mix, g_pre_ffn, w_up, w_down, g_post_ffn):
    bsz, t, d = x_prompt.shape
    nb, ts, _ = x_sample.shape
    depth = w_in.shape[0]
    cc = conv_w.shape[2]
    n_pages = page_table.shape[1]
    past = n_pages * PAGE_SIZE
    wb = cache_k_win.shape[2]
    assert ts == 1 and depth == 1 and KV_W == LANES and past % SEL_BLOCK == 0 and wb == min(WINDOW, past)
    assert min(WINDOW, t) == WINDOW
    l = 0
    pos_p = jnp.arange(t)
    pos_s = jnp.full((nb,), past, jnp.int32)

    w_cat = _prep_w_in(w_in[l], cc)
    w_out_c, w_out_o = _prep_w_out(w_out[l], cc)
    w_up_b = w_up[l].astype(_MXU_DTYPE)
    w_down_b = w_down[l].astype(_MXU_DTYPE)
    wk = _compress_weights(cmp_k_w1[l], cmp_k_b1[l], cmp_k_w2[l], cmp_k_pe[l])
    wv = _compress_weights(cmp_v_w1[l], cmp_v_b1[l], cmp_v_w2[l], cmp_v_pe[l])
    out_w = (w_out_c, w_out_o, g_post_mix[l], g_pre_ffn[l], w_up_b, w_down_b, g_post_ffn[l])
    kv5 = lambda a, n, m: a.reshape(1, n, m, N_KV, HEAD_DIM)

    (u, q, kc, vc, ks, vs, kw, vw, ksa, vsb, kwb, vwb, gates) = _inproj(
        x_prompt.reshape(bsz * t, d), pos_p, g_pre_mix[l], w_cat, cc)
    u3 = u.reshape(bsz, t, cc)
    cy = _conv_prompt(u3, conv_w[l], conv_b[l], conv_ln_g[l], conv_ln_b[l])
    b3 = lambda a: a.reshape(bsz, t, a.shape[-1])
    kcc, vcc = _cmp_prompt(b3(kc), b3(vc), wk, wv)
    o = _nsa_prompt(b3(q), b3(gates), kcc, vcc, b3(ksa), b3(vsb), b3(kwb), b3(vwb))
    y_p = _layer_out(x_prompt.reshape(bsz * t, d), cy.reshape(bsz * t, cc), o.reshape(bsz * t, -1), *out_w)
    y_p = y_p.reshape(bsz, t, d)
    p_states = (kv5(kc, bsz, t), kv5(vc, bsz, t), kv5(ks, bsz, t), kv5(vs, bsz, t),
                kv5(b3(kw)[:, -WINDOW:], bsz, WINDOW), kv5(b3(vw)[:, -WINDOW:], bsz, WINDOW),
                u3[:, -(CONV_K - 1):][None])

    (u_s, q_s, kc_s, vc_s, ks_s, vs_s, kw_s, vw_s, _, _, _, _, gates_s) = _inproj(
        x_sample.reshape(nb, d), pos_s, g_pre_mix[l], w_cat, cc)
    cy_s = _conv_sample(cache_conv[l], u_s, conv_w[l], conv_b[l], conv_ln_g[l], conv_ln_b[l])
    pool = lambda a: a[l].reshape(a.shape[1] * PAGE_ROWS, HEAD_DIM)
    wkp = _paged_compress_weights(cmp_k_w1[l], cmp_k_b1[l], cmp_k_w2[l], cmp_k_pe[l])
    wvp = _paged_compress_weights(cmp_v_w1[l], cmp_v_b1[l], cmp_v_w2[l], cmp_v_pe[l])
    kvc = _cmp_sample(page_table, pool(cache_k_cmp), pool(cache_v_cmp), wkp, wvp)
    q8 = q_s.astype(F32).reshape(nb, GQA_R, N_KV, HEAD_DIM).transpose(0, 2, 1, 3).reshape(nb, N_HEADS, HEAD_DIM)
    zq = jnp.zeros((nb, GQA_R, HEAD_DIM), F32)
    q_pad = jnp.concatenate([jnp.concatenate([q8[:, :GQA_R], zq], -1), jnp.concatenate([zq, q8[:, GQA_R:]], -1)], 1)
    o_c, picks = _sample_select(q_pad, kvc, past // SEL_BLOCK)
    idx = picks[:, :N_SEL]
    win_rows = lambda a: a[l].reshape(nb, wb * N_KV, HEAD_DIM)
    o_s = _sample_attend(idx, page_table, q8, gates_s, o_c, ks_s, vs_s, kw_s, vw_s, win_rows(cache_k_win),
                         win_rows(cache_v_win), pool(cache_k_sel), pool(cache_v_sel))
    out_w_s = (w_out_c, w_out[l][cc:].astype(_MXU_DTYPE)) + out_w[2:]
    y_s = _layer_out(x_sample.reshape(nb, d), cy_s.astype(_MXU_DTYPE), o_s.reshape(nb, -1).astype(_MXU_DTYPE),
                     *out_w_s)
    y_s = y_s.reshape(nb, 1, d)
    new_win = lambda buf, new: jnp.concatenate([buf[l], new.reshape(nb, 1, N_KV, HEAD_DIM)], 1)[:, -wb:][None]
    s_conv = jnp.concatenate([cache_conv[l], u_s[:, None, :]], 1)[:, -(CONV_K - 1):][None]
    s_states = (kv5(kc_s, nb, 1), kv5(vc_s, nb, 1), kv5(ks_s, nb, 1), kv5(vs_s, nb, 1),
                new_win(cache_k_win, kw_s), new_win(cache_v_win, vw_s), s_conv)
    return (y_p, y_s) + p_states + s_states
```

```python
import functools

import numpy as np
import jax
import jax.numpy as jnp
from jax import lax
from jax.experimental import pallas as pl
from jax.experimental.pallas import tpu as pltpu

HEAD_DIM = 64
N_KV = 2
GQA_R = 4
N_HEADS = N_KV * GQA_R
CONV_K = 31
ROT_DIM = 16
ROPE_THETA = 500000.0
CMP_LEN = 32
CMP_STRIDE = 16
CMP_HIDDEN = 2 * HEAD_DIM
SEL_BLOCK = 64
N_SEL = 16
WINDOW = 512
PAGE_SIZE = 128
EPS = 1e-6
NEG = -1e30
SEL_OFF = -(2.0 ** 100)
LOG2E = 1.4426950408889634
LANES = 128
KV_W = N_KV * HEAD_DIM
N_GATE = 3 * N_HEADS
HALO = 32
SEL_CHUNK = 512
VMEM_LIMIT = 56 * 1024 * 1024

_MXU_DTYPE = jnp.bfloat16
F32 = jnp.float32


def _dot(a, b):
    return jnp.dot(a, b, preferred_element_type=F32)


def _dot_nt(a, b):
    return lax.dot_general(a, b, (((1,), (1,)), ((), ())), preferred_element_type=F32)


def _rms(x, g):
    return x * lax.rsqrt(jnp.mean(x * x, -1, keepdims=True) + EPS) * g


def _const_spec(shape):
    n = len(shape)
    return pl.BlockSpec(shape, lambda *_: (0,) * n, pipeline_mode=pl.Buffered(1))


def _params(sem):
    return pltpu.CompilerParams(dimension_semantics=sem, vmem_limit_bytes=VMEM_LIMIT)


def _inproj_kernel(x_ref, g_ref, w_ref, cos_ref, sa_ref, sb_ref, oh_ref,
                   u_ref, q_ref, kc_ref, vc_ref, ks_ref, vs_ref, kw_ref, vw_ref,
                   ksa_ref, vsb_ref, kwb_ref, vwb_ref, gate_ref):
    cc = u_ref.shape[1]
    xb = _rms(x_ref[...], g_ref[...]).astype(_MXU_DTYPE)
    cos, sa, sb = cos_ref[...], sa_ref[...], sb_ref[...]

    def seg(lo, hi):
        return _dot(xb, w_ref[:, lo:hi])

    def rope(v):
        return v * cos + pltpu.roll(v, LANES - ROT_DIM // 2, 1) * sa + pltpu.roll(v, ROT_DIM // 2, 1) * sb

    z = seg(0, 2 * cc)
    u_ref[...] = z[:, :cc] * jax.nn.sigmoid(z[:, cc:])
    c0 = 2 * cc
    zq = seg(c0, c0 + N_HEADS * HEAD_DIM)
    for c in range(GQA_R):
        q_ref[:, c * LANES:(c + 1) * LANES] = rope(zq[:, c * LANES:(c + 1) * LANES]).astype(q_ref.dtype)
    c1 = c0 + N_HEADS * HEAD_DIM
    zkv = seg(c1, c1 + 6 * KV_W)
    kc = rope(zkv[:, 0 * KV_W:1 * KV_W])
    vc = zkv[:, 1 * KV_W:2 * KV_W]
    ks = rope(zkv[:, 2 * KV_W:3 * KV_W])
    vs = zkv[:, 3 * KV_W:4 * KV_W]
    kw = rope(zkv[:, 4 * KV_W:5 * KV_W])
    vw = zkv[:, 5 * KV_W:6 * KV_W]
    kc_ref[0] = kc.T
    vc_ref[0] = vc.T
    ks_ref[0] = ks.T
    vs_ref[0] = vs.T
    kw_ref[0] = kw.T
    vw_ref[0] = vw.T
    ksa_ref[:, :KV_W] = ks.astype(ksa_ref.dtype)
    ksa_ref[:, KV_W:] = oh_ref[...]
    vsb_ref[...] = vs.astype(vsb_ref.dtype)
    kwb_ref[...] = kw.astype(kwb_ref.dtype)
    vwb_ref[...] = vw.astype(vwb_ref.dtype)
    c2 = c1 + 6 * KV_W
    gate_ref[...] = jax.nn.sigmoid(seg(c2, c2 + LANES))


def _rope_tables(pos):
    half = ROT_DIM // 2
    inv = jnp.power(jnp.float32(ROPE_THETA), -jnp.arange(half, dtype=jnp.float32) * 2.0 / ROT_DIM)
    ang = pos.astype(jnp.float32)[:, None] * inv[None, :]
    cos, sin = jnp.cos(ang), jnp.sin(ang)
    n = pos.shape[0]
    ones = jnp.ones((n, HEAD_DIM - ROT_DIM), F32)
    zeros8 = jnp.zeros((n, half), F32)
    zeros_rest = jnp.zeros((n, HEAD_DIM - ROT_DIM), F32)
    cos_h = jnp.concatenate([cos, cos, ones], 1)
    sa_h = jnp.concatenate([-sin, zeros8, zeros_rest], 1)
    sb_h = jnp.concatenate([zeros8, sin, zeros_rest], 1)
    two = lambda a: jnp.concatenate([a, a], 1)
    return two(cos_h), two(sa_h), two(sb_h)


def _block_onehot(pos):
    j = jnp.arange(LANES)[None, :]
    return ((pos[:, None] // SEL_BLOCK) == j).astype(_MXU_DTYPE)


def _inproj(x2d, pos, g_pre, w_cat, cc):
    n, d = x2d.shape
    period = pos.shape[0]
    tile = 512 if (n % 512 == 0 and period % 512 == 0) else period
    assert n % tile == 0 and period % tile == 0
    tpp = period // tile
    cos, sa, sb = _rope_tables(pos)
    oh = _block_onehot(pos)
    tok = lambda w: pl.BlockSpec((tile, w), lambda i: (i, 0))
    tab = lambda w: pl.BlockSpec((tile, w), lambda i: (i % tpp, 0))
    f32o = lambda w: jax.ShapeDtypeStruct((n, w), F32)
    b16o = lambda w: jax.ShapeDtypeStruct((n, w), _MXU_DTYPE)
    cache_o = jax.ShapeDtypeStruct((n // period, KV_W, period), F32)
    cache_spec = pl.BlockSpec((1, KV_W, tile), lambda i: (i // tpp, 0, i % tpp))
    out_shape = (f32o(cc), b16o(N_HEADS * HEAD_DIM)) + (cache_o,) * 6 + (
        b16o(2 * KV_W), b16o(KV_W), b16o(KV_W), b16o(KV_W), f32o(LANES))
    out_specs = (tok(cc), tok(N_HEADS * HEAD_DIM)) + (cache_spec,) * 6 + (
        tok(2 * KV_W), tok(KV_W), tok(KV_W), tok(KV_W), tok(LANES))
    return pl.pallas_call(
        _inproj_kernel,
        grid=(n // tile,),
        in_specs=[tok(d), _const_spec((1, d)), _const_spec(w_cat.shape), tab(LANES), tab(LANES), tab(LANES),
                  tab(LANES)],
        out_specs=out_specs,
        out_shape=out_shape,
        compiler_params=_params(("parallel",)),
        name="inproj",
    )(x2d, g_pre.reshape(1, d), w_cat, cos, sa, sb, oh)


def _ln_silu(y, g, b):
    yc = y - jnp.mean(y, -1, keepdims=True)
    yn = yc * lax.rsqrt(jnp.mean(yc * yc, -1, keepdims=True) + EPS) * g + b
    return yn * jax.nn.sigmoid(yn)


def _conv_prompt_kernel(u_ref, halo_ref, w_ref, b_ref, g_ref, bb_ref, o_ref, ext_ref, *, rows):
    i = pl.program_id(1)
    tc = u_ref.shape[1]
    ext_ref[0:HALO, :] = jnp.where(i == 0, 0.0, halo_ref[0])
    ext_ref[HALO:, :] = u_ref[0]
    lead = HALO - (CONV_K - 1)

    for c in range(tc // rows):
        base = c * rows
        acc = jnp.zeros((rows, u_ref.shape[2]), F32)
        for r in range(8):
            taps = [k for k in range(CONV_K) if (lead + k) % 8 == r]
            span = max((lead + k) // 8 for k in taps) * 8 + rows
            win = ext_ref[base + r:base + r + span, :]
            for k in taps:
                a = (lead + k) // 8 * 8
                acc = acc + win[a:a + rows] * w_ref[k:k + 1, :]
        y = _ln_silu(acc + b_ref[...], g_ref[...], bb_ref[...])
        o_ref[0, base:base + rows, :] = y.astype(o_ref.dtype)


def _conv_prompt(u, conv_w, conv_b, ln_g, ln_b):
    bsz, t, cc = u.shape
    tc = 512 if t % 512 == 0 else t
    rows = 64
    assert t % tc == 0 and tc % rows == 0 and tc % HALO == 0
    hb = tc // HALO
    vec = lambda a: a.reshape(1, cc)
    return pl.pallas_call(
        functools.partial(_conv_prompt_kernel, rows=rows),
        grid=(bsz, t // tc),
        in_specs=[pl.BlockSpec((1, tc, cc), lambda b, i: (b, i, 0)),
                  pl.BlockSpec((1, HALO, cc), lambda b, i: (b, jnp.maximum(i * hb - 1, 0), 0)),
                  _const_spec((CONV_K, cc)), _const_spec((1, cc)), _const_spec((1, cc)), _const_spec((1, cc))],
        out_specs=pl.BlockSpec((1, tc, cc), lambda b, i: (b, i, 0)),
        out_shape=jax.ShapeDtypeStruct((bsz, t, cc), _MXU_DTYPE),
        scratch_shapes=[pltpu.VMEM((tc + HALO, cc), F32)],
        compiler_params=_params(("parallel", "parallel")),
        name="conv_prompt",
    )(u, u, conv_w, vec(conv_b), vec(ln_g), vec(ln_b))


def _conv_sample_kernel(cache_ref, u_ref, w_ref, b_ref, g_ref, bb_ref, o_ref):
    nb = cache_ref.shape[0]
    w_hist = w_ref[0:CONV_K - 1, :]
    rows = [jnp.sum(cache_ref[i] * w_hist, axis=0, keepdims=True) for i in range(nb)]
    y = jnp.concatenate(rows, 0) + u_ref[...] * w_ref[CONV_K - 1:CONV_K, :] + b_ref[...]
    o_ref[...] = _ln_silu(y, g_ref[...], bb_ref[...])


def _conv_sample(cache_conv, u, conv_w, conv_b, ln_g, ln_b):
    nb, hist, cc = cache_conv.shape
    blk = 8
    assert nb % blk == 0 and hist == CONV_K - 1
    vec = lambda a: a.reshape(1, cc)
    return pl.pallas_call(
        _conv_sample_kernel,
        grid=(nb // blk,),
        in_specs=[pl.BlockSpec((blk, hist, cc), lambda i: (i, 0, 0)), pl.BlockSpec((blk, cc), lambda i: (i, 0)),
                  _const_spec((CONV_K, cc)), _const_spec((1, cc)), _const_spec((1, cc)), _const_spec((1, cc))],
        out_specs=pl.BlockSpec((blk, cc), lambda i: (i, 0)),
        out_shape=jax.ShapeDtypeStruct((nb, cc), F32),
        compiler_params=_params(("parallel",)),
        name="conv_sample",
    )(cache_conv, u, conv_w, vec(conv_b), vec(ln_g), vec(ln_b))


CMP_GROUP = 16


def _compress_pages(page, n_pages, rows_ref, acc_ref, wbig_ref, bias_ref, w2_ref):
    per_group = min(CMP_GROUP, n_pages)
    gch = per_group * PAGE_SIZE // CMP_STRIDE
    nch = n_pages * PAGE_SIZE // CMP_STRIDE
    for q in range(n_pages // per_group):
        for p in range(q * per_group, (q + 1) * per_group):
            rows_ref[p * PAGE_SIZE:(p + 1) * PAGE_SIZE, :] = page(p).T
        r0 = q * per_group * PAGE_SIZE
        acc = None
        for jj in range(CMP_STRIDE // 2):
            a = rows_ref[pl.ds(r0 + 2 * jj, gch, stride=CMP_STRIDE), :]
            b = rows_ref[pl.ds(r0 + 2 * jj + 1, gch, stride=CMP_STRIDE), :]
            d = _dot(jnp.concatenate([a, b], axis=1).astype(_MXU_DTYPE), wbig_ref[jj])
            acc = d if acc is None else acc + d
        acc_ref[q * gch:(q + 1) * gch, :] = acc
    hw = N_KV * CMP_HIDDEN
    first, second = acc_ref[:, :hw], acc_ref[:, hw:]
    second_next = pltpu.roll(second, nch - 1, 0)
    h = jax.nn.gelu(first + second_next + bias_ref[...], approximate=True)
    out = _dot(h.astype(_MXU_DTYPE), w2_ref[...])
    row = lax.broadcasted_iota(jnp.int32, out.shape, 0)
    return jnp.where(row < nch - 1, out, 0.0)


def _compress_weights(w1, b1, w2, pe):
    eye = jnp.eye(N_KV, dtype=F32)
    w1s = w1.reshape(2, CMP_STRIDE // 2, 2, HEAD_DIM, CMP_HIDDEN)
    wbig = jnp.einsum('sjldh,ge->jlgdseh', w1s, eye)
    wbig = wbig.reshape(CMP_STRIDE // 2, 2 * KV_W, 2 * N_KV * CMP_HIDDEN).astype(_MXU_DTYPE)
    bias = b1 + jnp.einsum('jd,jdh->h', pe, w1, precision=lax.Precision.HIGHEST)
    bias2 = jnp.tile(bias, N_KV).reshape(1, N_KV * CMP_HIDDEN)
    w2big = jnp.einsum('hd,ge->ghed', w2, eye).reshape(N_KV * CMP_HIDDEN, KV_W).astype(_MXU_DTYPE)
    return wbig, bias2, w2big


def _cmp_prompt_kernel(rk_ref, rv_ref, wk_ref, bk_ref, w2k_ref, wv_ref, bv_ref, w2v_ref, kc_ref, vc_ref,
                       rows_ref, acc_ref):
    n_pages = rk_ref.shape[2] // PAGE_SIZE
    for src, w, bias, w2, dst in ((rk_ref, wk_ref, bk_ref, w2k_ref, kc_ref), (rv_ref, wv_ref, bv_ref, w2v_ref, vc_ref)):
        page = lambda p, src=src: src[0, :, p * PAGE_SIZE:(p + 1) * PAGE_SIZE]
        dst[0] = _compress_pages(page, n_pages, rows_ref, acc_ref, w, bias, w2).astype(dst.dtype)


def _cmp_prompt(rows_k, rows_v, wk, wv):
    bsz, _, t = rows_k.shape
    nch = t // CMP_STRIDE
    assert t % PAGE_SIZE == 0
    wspecs = [_const_spec(a.shape) for a in wk]
    row_spec = pl.BlockSpec((1, KV_W, t), lambda b: (b, 0, 0))
    out_spec = pl.BlockSpec((1, nch, KV_W), lambda b: (b, 0, 0))
    out = jax.ShapeDtypeStruct((bsz, nch, KV_W), _MXU_DTYPE)
    return pl.pallas_call(
        _cmp_prompt_kernel,
        grid=(bsz,),
        in_specs=[row_spec, row_spec] + wspecs + wspecs,
        out_specs=(out_spec, out_spec),
        out_shape=(out, out),
        scratch_shapes=[pltpu.VMEM((t, KV_W), F32), pltpu.VMEM((nch, 2 * N_KV * CMP_HIDDEN), F32)],
        compiler_params=_params(("parallel",)),
        name="cmp_prompt",
    )(rows_k, rows_v, *wk, *wv)


def _cmp_sample_kernel(pt_ref, pk_ref, pv_ref, w1_ref, b_ref, w2_ref, out_ref, buf, rows_ref, acc_ref, sem, *,
                       n_pages):
    b = pl.program_id(0)
    s = pl.program_id(1)

    def copies(pool_ref, seq, slot, wait):
        for p in range(n_pages):
            cp = pltpu.make_async_copy(pool_ref.at[0 if wait else pt_ref[seq, p]], buf.at[slot, p], sem.at[slot])
            if wait:
                cp.wait()
            else:
                cp.start()

    @pl.when((b == 0) & (s == 0))
    def _():
        copies(pk_ref, 0, 0, False)

    @pl.when(s == 0)
    def _():
        copies(pv_ref, b, 1, False)

    @pl.when((s == 1) & (b + 1 < pl.num_programs(0)))
    def _():
        copies(pk_ref, b + 1, 0, False)

    copies(pk_ref, b, s, True)

    page = lambda p: buf[s, p]
    out = _compress_pages(page, n_pages, rows_ref, acc_ref, w1_ref.at[0], b_ref.at[0], w2_ref.at[0])
    out_ref[0, 0] = out.astype(out_ref.dtype)


def _cmp_sample(page_table, pool_k, pool_v, wk, wv):
    nb, n_pages = page_table.shape
    past = n_pages * PAGE_SIZE
    nch = past // CMP_STRIDE
    w1, bias, w2 = (jnp.stack([a, c]) for a, c in zip(wk, wv))
    sel = lambda a: pl.BlockSpec((1,) + a.shape[1:], lambda b, s, pt, n=a.ndim: (s,) + (0,) * (n - 1))
    any_spec = pl.BlockSpec(memory_space=pl.ANY)
    return pl.pallas_call(
        functools.partial(_cmp_sample_kernel, n_pages=n_pages),
        grid_spec=pltpu.PrefetchScalarGridSpec(
            num_scalar_prefetch=1,
            grid=(nb, 2),
            in_specs=[any_spec, any_spec, sel(w1), sel(bias), sel(w2)],
            out_specs=pl.BlockSpec((1, 1, nch, KV_W), lambda b, s, pt: (s, b, 0, 0)),
            scratch_shapes=[pltpu.VMEM((2, n_pages, KV_W, PAGE_SIZE), F32), pltpu.VMEM((past, KV_W), F32),
                            pltpu.VMEM((nch, 2 * N_KV * CMP_HIDDEN), F32), pltpu.SemaphoreType.DMA((2,))]),
        out_shape=jax.ShapeDtypeStruct((2, nb, nch, KV_W), _MXU_DTYPE),
        compiler_params=_params(("arbitrary", "arbitrary")),
        name="cmp_sample",
    )(page_table, pool_k, pool_v, w1, bias, w2)


def _gate_expand_matrix():
    e = np.zeros((2 * LANES, 3 * GQA_R * LANES), np.float32)
    for br in range(3):
        for g in range(N_KV):
            for r in range(GQA_R):
                c = br * N_HEADS + g * GQA_R + r
                lo = (br * GQA_R + r) * LANES + g * HEAD_DIM
                e[c, lo:lo + HEAD_DIM] = 1.0
                e[LANES + c, lo:lo + HEAD_DIM] = 1.0
    return jnp.asarray(e, dtype=_MXU_DTYPE)


def _expand_gates(gt, e_ref):
    hi = gt.astype(_MXU_DTYPE)
    lo = (gt - hi.astype(F32)).astype(_MXU_DTYPE)
    return _dot(jnp.concatenate([hi, lo], axis=1), e_ref[...])


def _overlap_t(n_cmp_pad, n_blk_pad, n_cmp, n_blk):
    i = np.arange(n_cmp_pad)[None, :]
    j = np.arange(n_blk_pad)[:, None]
    lo = np.maximum(i * CMP_STRIDE, j * SEL_BLOCK)
    hi = np.minimum(i * CMP_STRIDE + CMP_LEN, (j + 1) * SEL_BLOCK)
    ov = (np.maximum(hi - lo, 0) // CMP_STRIDE).astype(np.float32)
    ov = ov * (i < n_cmp) * (j < n_blk)
    return ov


def _nsa_prompt_kernel(q_ref, gate_ref, kc_ref, vc_ref, ksa_ref, vs_ref, kw_ref, vw_ref, ovt_ref, e_ref, o_ref,
                       m_ref, l_ref, acc_ref, *, seq_len):
    tq = q_ref.shape[1]
    rows = GQA_R * tq
    start = pl.program_id(1) * tq
    qt = q_ref[0]
    lane = lax.broadcasted_iota(jnp.int32, (tq, LANES), 1)
    gx = _expand_gates(gate_ref[0], e_ref)
    kc = kc_ref[0]
    vc = vc_ref[0]
    n_cmp = kc.shape[0]
    n_sel_rows = SEL_BLOCK

    def tok(shape):
        return start + (lax.broadcasted_iota(jnp.int32, shape, 0) & (tq - 1))

    def rep(v, n):
        return jnp.concatenate([v] * n, axis=1) if n > 1 else v

    wk = WINDOW + tq
    ws = pl.multiple_of(jnp.clip(start - WINDOW, 0, seq_len - wk), tq)
    n_full = start // SEL_CHUNK

    branch = [[None] * N_KV for _ in range(3)]
    for g in range(N_KV):
        mine = (lane < HEAD_DIM) if g == 0 else (lane >= HEAD_DIM)
        lhs = jnp.concatenate(
            [jnp.where(mine, qt[:, r * LANES:(r + 1) * LANES], jnp.zeros_like(qt[:, :LANES])) for r in range(GQA_R)],
            axis=0)

        s = _dot_nt(lhs, kc)
        vis = lax.broadcasted_iota(jnp.int32, s.shape, 1) * CMP_STRIDE + (CMP_LEN - 1) <= tok(s.shape)
        sm = jnp.where(vis, s, NEG)
        e = jnp.where(vis, jnp.exp2(sm - jnp.max(sm, -1, keepdims=True)), 0.0)
        p = e * (1.0 / jnp.maximum(jnp.sum(e, -1, keepdims=True), 1e-30))
        branch[0][g] = _dot(p.astype(_MXU_DTYPE), vc)
        psum = (p[0:tq] + p[tq:2 * tq]) + p[2 * tq:3 * tq] + p[3 * tq:4 * tq]
        imp = _dot_nt(ovt_ref[...], psum.astype(_MXU_DTYPE))[:n_sel_rows]

        j = lax.broadcasted_iota(jnp.int32, imp.shape, 0)
        cur = (start + lax.broadcasted_iota(jnp.int32, imp.shape, 1)) // SEL_BLOCK
        future = j > cur
        forced = (j == 0) | (j == cur) | (j == cur - 1)
        score = jnp.where(future, -jnp.inf, jnp.where(forced, jnp.inf, imp))
        slabs = [score[8 * v:8 * v + 8] for v in range(n_sel_rows // 8)]
        ranks = [jnp.zeros((8, tq), jnp.int32) for _ in slabs]
        sub = lax.broadcasted_iota(jnp.int32, (8, tq), 0)
        for i in range(n_sel_rows):
            si = score[i:i + 1, :]
            for v, sl in enumerate(slabs):
                if v < i // 8:
                    inc = jnp.where(si > sl, 1, 0)
                elif v > i // 8:
                    inc = jnp.where(si >= sl, 1, 0)
                else:
                    inc = jnp.where(sub > i % 8, jnp.where(si >= sl, 1, 0), jnp.where(si > sl, 1, 0))
                ranks[v] = ranks[v] + inc
        rank = jnp.concatenate(ranks, axis=0)
        bias_t = jnp.where(future, SEL_OFF, jnp.where(rank < N_SEL, 0.0, SEL_OFF))
        bias_t = jnp.concatenate([bias_t, jnp.zeros((LANES - n_sel_rows, tq), F32)], axis=0)
        bias = jnp.transpose(bias_t).astype(_MXU_DTYPE)
        lhs2 = jnp.concatenate([lhs, jnp.concatenate([bias] * GQA_R, axis=0)], axis=1)

        m_ref[...] = jnp.full(m_ref.shape, NEG, F32)
        l_ref[...] = jnp.zeros(l_ref.shape, F32)
        acc_ref[...] = jnp.zeros(acc_ref.shape, F32)

        def sel_step(c, carry, lhs2=lhs2, causal=False):
            k0 = pl.multiple_of(c * SEL_CHUNK, SEL_CHUNK)
            s = _dot_nt(lhs2, ksa_ref[0, pl.ds(k0, SEL_CHUNK), :])
            if causal:
                kpos = k0 + lax.broadcasted_iota(jnp.int32, s.shape, 1)
                s = jnp.where(kpos <= tok(s.shape), s, NEG)
            m_prev = m_ref[...]
            m_new = jnp.maximum(m_prev, jnp.max(s, -1, keepdims=True))
            alpha = jnp.exp2(m_prev - m_new)
            pe = jnp.exp2(s - rep(m_new, SEL_CHUNK // LANES))
            l_ref[...] = alpha * l_ref[...] + jnp.sum(pe, -1, keepdims=True)
            acc_ref[...] = alpha * acc_ref[...] + _dot(pe.astype(_MXU_DTYPE), vs_ref[0, pl.ds(k0, SEL_CHUNK), :])
            m_ref[...] = m_new
            return carry

        lax.fori_loop(0, n_full, sel_step, 0)
        sel_step(n_full, 0, causal=True)
        branch[1][g] = acc_ref[...] / l_ref[...]

        s = _dot_nt(lhs, kw_ref[0, pl.ds(ws, wk), :])
        kpos = ws + lax.broadcasted_iota(jnp.int32, s.shape, 1)
        t = tok(s.shape)
        s = jnp.where(kpos <= t, jnp.where(kpos >= t - WINDOW, s, NEG), NEG)
        e = jnp.exp2(s - jnp.max(s, -1, keepdims=True))
        p = e * (1.0 / jnp.sum(e, -1, keepdims=True))
        branch[2][g] = _dot(p.astype(_MXU_DTYPE), vw_ref[0, pl.ds(ws, wk), :])

    first_half = lane < HEAD_DIM
    for r in range(GQA_R):
        out = jnp.zeros((tq, LANES), F32)
        for br in range(3):
            both = jnp.where(first_half, branch[br][0][r * tq:(r + 1) * tq], branch[br][1][r * tq:(r + 1) * tq])
            col = (br * GQA_R + r) * LANES
            out = out + gx[:, col:col + LANES] * both
        o_ref[0, :, r * LANES:(r + 1) * LANES] = out.astype(o_ref.dtype)


def _nsa_prompt(q, gates, kc, vc, ksa, vsb, kwb, vwb):
    bsz, t, qw = q.shape
    tq = 128
    n_cmp = kc.shape[1]
    assert t % SEL_CHUNK == 0 and t >= WINDOW + tq and t // SEL_BLOCK <= SEL_BLOCK and qw == GQA_R * LANES
    ovt = jnp.asarray(_overlap_t(n_cmp, LANES, n_cmp - 1, t // SEL_BLOCK), dtype=_MXU_DTYPE)
    e = _gate_expand_matrix()
    tile = lambda w: pl.BlockSpec((1, tq, w), lambda b, i: (b, i, 0))
    full = lambda n, w: pl.BlockSpec((1, n, w), lambda b, i: (b, 0, 0))
    rows = GQA_R * tq
    return pl.pallas_call(
        functools.partial(_nsa_prompt_kernel, seq_len=t),
        grid=(bsz, t // tq),
        in_specs=[tile(qw), tile(LANES), full(n_cmp, KV_W), full(n_cmp, KV_W), full(t, 2 * KV_W), full(t, KV_W),
                  full(t, KV_W), full(t, KV_W), _const_spec(ovt.shape), _const_spec(e.shape)],
        out_specs=tile(qw),
        out_shape=jax.ShapeDtypeStruct((bsz, t, qw), _MXU_DTYPE),
        scratch_shapes=[pltpu.VMEM((rows, LANES), F32)] * 3,
        compiler_params=_params(("parallel", "arbitrary")),
        name="nsa_prompt",
    )(q, gates, kc, vc, ksa, vsb, kwb, vwb, ovt, e)


SEQ_BLK = 8


def _sample_select_kernel(q_ref, kc_ref, vc_ref, ov_ref, oc_ref, idx_ref, *, n_vis, n_past_blk):
    psums = []
    for bi in range(SEQ_BLK):
        s = _dot_nt(q_ref[bi].astype(_MXU_DTYPE), kc_ref[0, bi])
        vis = lax.broadcasted_iota(jnp.int32, s.shape, 1) < n_vis
        sm = jnp.where(vis, s, NEG)
        e = jnp.where(vis, jnp.exp2(sm - jnp.max(sm, -1, keepdims=True)), 0.0)
        p = e * (1.0 / jnp.maximum(jnp.sum(e, -1, keepdims=True), 1e-30))
        oc_ref[bi] = _dot(p.astype(_MXU_DTYPE), vc_ref[0, bi])
        row = lax.broadcasted_iota(jnp.int32, p.shape, 0)
        psums.append(jnp.sum(jnp.where(row < GQA_R, p, 0.0), axis=0, keepdims=True))
        psums.append(jnp.sum(jnp.where(row >= GQA_R, p, 0.0), axis=0, keepdims=True))
    psum = jnp.concatenate(psums, axis=0)
    imp = _dot(psum.astype(_MXU_DTYPE), ov_ref[...])
    j = lax.broadcasted_iota(jnp.int32, imp.shape, 1)
    score = jnp.where(j >= n_past_blk, -jnp.inf,
                      jnp.where(j == 0, jnp.inf, jnp.where(j == n_past_blk - 1, jnp.inf, imp)))
    picks = jnp.zeros(imp.shape, jnp.int32)
    for it in range(N_SEL - 1):
        best = jnp.max(score, -1, keepdims=True)
        pick = jnp.min(jnp.where(score == best, j, LANES), -1, keepdims=True)
        picks = jnp.where(j == it, pick, picks)
        score = jnp.where(j == pick, -jnp.inf, score)
    idx_ref[...] = picks


def _sample_select(q_pad, kvc, n_past_blk):
    nb = q_pad.shape[0]
    n_cmp_pad = kvc.shape[2]
    assert n_past_blk <= LANES and n_past_blk >= N_SEL - 1 and nb % SEQ_BLK == 0
    ov = jnp.asarray(_overlap_t(n_cmp_pad, LANES, n_cmp_pad - 1, n_past_blk).T, dtype=_MXU_DTYPE)
    return pl.pallas_call(
        functools.partial(_sample_select_kernel, n_vis=n_cmp_pad - 1, n_past_blk=n_past_blk),
        grid=(nb // SEQ_BLK,),
        in_specs=[pl.BlockSpec((SEQ_BLK, N_HEADS, LANES), lambda i: (i, 0, 0)),
                  pl.BlockSpec((1, SEQ_BLK, n_cmp_pad, KV_W), lambda i: (0, i, 0, 0)),
                  pl.BlockSpec((1, SEQ_BLK, n_cmp_pad, KV_W), lambda i: (1, i, 0, 0)),
                  _const_spec(ov.shape)],
        out_specs=(pl.BlockSpec((SEQ_BLK, N_HEADS, KV_W), lambda i: (i, 0, 0)),
                   pl.BlockSpec((SEQ_BLK * N_KV, LANES), lambda i: (i, 0))),
        out_shape=(jax.ShapeDtypeStruct((nb, N_HEADS, KV_W), F32),
                   jax.ShapeDtypeStruct((nb * N_KV, LANES), jnp.int32)),
        compiler_params=_params(("parallel",)),
        name="sample_select",
    )(q_pad, kvc, kvc, ov)


BLK_PER_PAGE = PAGE_SIZE // SEL_BLOCK


def _block_copies(idx_ref, pt_ref, pool_ref, buf_ref, sem, seq, slot, wait):
    for g in range(N_KV):
        for i in range(N_SEL - 1):
            page = 0 if wait else pt_ref[seq, idx_ref[seq * N_KV + g, i] // BLK_PER_PAGE]
            cp = pltpu.make_async_copy(pool_ref.at[page], buf_ref.at[slot, g, i], sem)
            if wait:
                cp.wait()
            else:
                cp.start()


def _softmax_with_new(q, s, k_new):
    kn = k_new.astype(_MXU_DTYPE).astype(F32)
    s_new = jnp.sum(q.astype(F32) * kn, -1, keepdims=True)
    m = jnp.maximum(jnp.max(s, -1, keepdims=True), s_new)
    e = jnp.exp2(s - m)
    e_new = jnp.exp2(s_new - m)
    return e, e_new, jnp.sum(e, -1, keepdims=True) + e_new


def _sample_attend_kernel(idx_ref, pt_ref, q_ref, gate_ref, oc_ref, ksn_ref, vsn_ref, kwn_ref, vwn_ref,
                          kwin_ref, vwin_ref, pks_ref, pvs_ref, o_ref, kbuf, vbuf, sem):
    b = pl.program_id(0)
    slot = b % 2

    def gather(seq, sl, wait):
        _block_copies(idx_ref, pt_ref, pks_ref, kbuf, sem.at[0, sl], seq, sl, wait)
        _block_copies(idx_ref, pt_ref, pvs_ref, vbuf, sem.at[1, sl], seq, sl, wait)

    @pl.when(b == 0)
    def _():
        gather(0, 0, False)

    @pl.when(b + 1 < pl.num_programs(0))
    def _():
        gather(b + 1, 1 - slot, False)

    q = q_ref[0].astype(_MXU_DTYPE)
    top = lax.broadcasted_iota(jnp.int32, (N_HEADS, 1), 0) < GQA_R
    cast = lambda a: a.astype(_MXU_DTYPE)

    s = _dot(q, cast(kwin_ref[0]))
    e, e_new, den = _softmax_with_new(q, s, kwn_ref[0])
    o_w = (_dot_nt(cast(e), cast(vwin_ref[0])) + e_new * vwn_ref[0].astype(_MXU_DTYPE).astype(F32)) / den

    gather(b, slot, True)
    half = lax.broadcasted_iota(jnp.int32, (N_HEADS, PAGE_SIZE), 1) // SEL_BLOCK
    parts = []
    for i in range(N_SEL - 1):
        per_g = []
        for g in range(N_KV):
            want = idx_ref[b * N_KV + g, i] % BLK_PER_PAGE
            per_g.append(jnp.where(half == want, _dot(q, cast(kbuf[slot, g, i])), NEG))
        parts.append(jnp.where(top, per_g[0], per_g[1]))
    s = jnp.concatenate(parts, axis=1)
    e, e_new, den = _softmax_with_new(q, s, ksn_ref[0])
    pv = jnp.zeros((N_HEADS, KV_W), F32)
    for i in range(N_SEL - 1):
        ei = cast(e[:, i * PAGE_SIZE:(i + 1) * PAGE_SIZE])
        pv = pv + jnp.where(top, _dot_nt(ei, cast(vbuf[slot, 0, i])), _dot_nt(ei, cast(vbuf[slot, 1, i])))
    o_s = (pv + e_new * vsn_ref[0].astype(_MXU_DTYPE).astype(F32)) / den

    gates = jnp.broadcast_to(gate_ref[0], (N_HEADS, LANES))
    col = lax.broadcasted_iota(jnp.int32, gates.shape, 1) - lax.broadcasted_iota(jnp.int32, gates.shape, 0)
    out = jnp.zeros((N_HEADS, KV_W), F32)
    for br, o_br in enumerate((oc_ref[0], o_s, o_w)):
        out = out + jnp.sum(jnp.where(col == br * N_HEADS, gates, 0.0), -1, keepdims=True) * o_br
    o_ref[0] = jnp.where(top, out[:, :HEAD_DIM], out[:, HEAD_DIM:])


def _sample_attend(idx, page_table, q_pad, gates, o_c, ks_new, vs_new, kw_new, vw_new, buf_k_win, buf_v_win,
                   pool_k_sel, pool_v_sel):
    nb = q_pad.shape[0]
    wb = buf_k_win.shape[2]
    per = lambda n, w: pl.BlockSpec((1, n, w), lambda b, *_: (b, 0, 0))
    any_spec = pl.BlockSpec(memory_space=pl.ANY)
    new = lambda a: a.reshape(nb, 1, KV_W)
    pages = pltpu.VMEM((2, N_KV, N_SEL - 1, KV_W, PAGE_SIZE), F32)
    return pl.pallas_call(
        _sample_attend_kernel,
        grid_spec=pltpu.PrefetchScalarGridSpec(
            num_scalar_prefetch=2,
            grid=(nb,),
            in_specs=[per(N_HEADS, KV_W), per(1, LANES), per(N_HEADS, KV_W)] + [per(1, KV_W)] * 4 + [
                per(KV_W, wb), per(KV_W, wb), any_spec, any_spec],
            out_specs=per(N_HEADS, HEAD_DIM),
            scratch_shapes=[pages, pages, pltpu.SemaphoreType.DMA((2, 2))]),
        out_shape=jax.ShapeDtypeStruct((nb, N_HEADS, HEAD_DIM), F32),
        compiler_params=_params(("arbitrary",)),
        name="sample_attend",
    )(idx, page_table, q_pad, gates.reshape(nb, 1, LANES), o_c, new(ks_new), new(vs_new), new(kw_new), new(vw_new),
      buf_k_win, buf_v_win, pool_k_sel, pool_v_sel)


def _layer_out_kernel(x_ref, cy_ref, o_ref, woc_ref, woo_ref, gpost_ref, gpre2_ref, wup_ref, wdn_ref, gpost2_ref,
                      y_ref, *, ff_chunk):
    mix = _dot(cy_ref[...], woc_ref[...]) + _dot(o_ref[...], woo_ref[...])
    h = x_ref[...] + _rms(mix, gpost_ref[...])
    hn = _rms(h, gpre2_ref[...]).astype(_MXU_DTYPE)
    f = None
    for c in range(wup_ref.shape[1] // ff_chunk):
        a = jnp.maximum(_dot(hn, wup_ref[:, c * ff_chunk:(c + 1) * ff_chunk]), 0.0)
        d = _dot((a * a).astype(_MXU_DTYPE), wdn_ref[c * ff_chunk:(c + 1) * ff_chunk, :])
        f = d if f is None else f + d
    y_ref[...] = h + _rms(f, gpost2_ref[...])


def _layer_out(x2d, cy, o, w_out_c, w_out_o, g_post, g_pre2, w_up, w_down, g_post2):
    n, d = x2d.shape
    tile = 512 if n % 512 == 0 else n
    tok = lambda w: pl.BlockSpec((tile, w), lambda i: (i, 0))
    vec = lambda a: a.reshape(1, d)
    return pl.pallas_call(
        functools.partial(_layer_out_kernel, ff_chunk=1024),
        grid=(n // tile,),
        in_specs=[tok(d), tok(cy.shape[1]), tok(o.shape[1]), _const_spec(w_out_c.shape), _const_spec(w_out_o.shape),
                  _const_spec((1, d)), _const_spec((1, d)), _const_spec(w_up.shape), _const_spec(w_down.shape),
                  _const_spec((1, d))],
        out_specs=tok(d),
        out_shape=jax.ShapeDtypeStruct((n, d), F32),
        compiler_params=_params(("parallel",)),
        name="layer_out",
    )(x2d, cy, o, w_out_c, w_out_o, vec(g_post), vec(g_pre2), w_up, w_down, vec(g_post2))


def _prep_w_in(w_in, cc):
    d = w_in.shape[0]
    c0 = 2 * cc
    c1 = c0 + N_HEADS * HEAD_DIM
    c2 = c1 + 6 * KV_W
    wq = w_in[:, c0:c1].reshape(d, N_KV, GQA_R, HEAD_DIM).transpose(0, 2, 1, 3).reshape(d, N_HEADS * HEAD_DIM)
    wq = wq * (HEAD_DIM ** -0.5 * LOG2E)
    wg = jnp.pad(w_in[:, c2:], ((0, 0), (0, LANES - N_GATE)))
    return jnp.concatenate([w_in[:, :c0], wq, w_in[:, c1:c2], wg], axis=1).astype(_MXU_DTYPE)


def _prep_w_out(w_out, cc):
    d = w_out.shape[1]
    wo = w_out[cc:].reshape(N_KV, GQA_R, HEAD_DIM, d).transpose(1, 0, 2, 3).reshape(N_HEADS * HEAD_DIM, d)
    return w_out[:cc].astype(_MXU_DTYPE), wo.astype(_MXU_DTYPE)


def kernel(x_prompt, x_sample, cache_k_cmp, cache_v_cmp, cache_k_sel, cache_v_sel, cache_k_win, cache_v_win, cache_conv, page_table, g_pre_mix, w_in, conv_w, conv_b, conv_ln_g, conv_ln_b, cmp_k_w1, cmp_k_b1, cmp_k_w2, cmp_k_pe, cmp_v_w1, cmp_v_b1, cmp_v_w2, cmp_v_pe, w_out, g_post_mix, g_pre_ffn, w_up, w_down, g_post_ffn):
    bsz, t, d = x_prompt.shape
    nb, ts, _ = x_sample.shape
    depth = w_in.shape[0]
    cc = conv_w.shape[2]
    n_pages = page_table.shape[1]
    past = n_pages * PAGE_SIZE
    wb = cache_k_win.shape[2]
    assert ts == 1 and depth == 1 and KV_W == LANES and past % SEL_BLOCK == 0 and wb == min(WINDOW, past)
    assert min(WINDOW, t) == WINDOW
    l = 0
    pos_p = jnp.arange(t)
    pos_s = jnp.full((nb,), past, jnp.int32)

    w_cat = _prep_w_in(w_in[l], cc)
    w_out_c, w_out_o = _prep_w_out(w_out[l], cc)
    w_up_b = w_up[l].astype(_MXU_DTYPE)
    w_down_b = w_down[l].astype(_MXU_DTYPE)
    wk = _compress_weights(cmp_k_w1[l], cmp_k_b1[l], cmp_k_w2[l], cmp_k_pe[l])
    wv = _compress_weights(cmp_v_w1[l], cmp_v_b1[l], cmp_v_w2[l], cmp_v_pe[l])
    out_w = (w_out_c, w_out_o, g_post_mix[l], g_pre_ffn[l], w_up_b, w_down_b, g_post_ffn[l])
    kv5 = lambda a, n, m: a.reshape(1, n, m, N_KV, HEAD_DIM)

    (u, q, kc, vc, ks, vs, kw, vw, ksa, vsb, kwb, vwb, gates) = _inproj(
        x_prompt.reshape(bsz * t, d), pos_p, g_pre_mix[l], w_cat, cc)
    u3 = u.reshape(bsz, t, cc)
    cy = _conv_prompt(u3, conv_w[l], conv_b[l], conv_ln_g[l], conv_ln_b[l])
    b3 = lambda a: a.reshape(bsz, t, a.shape[-1])
    kcc, vcc = _cmp_prompt(kc, vc, wk, wv)
    o = _nsa_prompt(b3(q), b3(gates), kcc, vcc, b3(ksa), b3(vsb), b3(kwb), b3(vwb))
    y_p = _layer_out(x_prompt.reshape(bsz * t, d), cy.reshape(bsz * t, cc), o.reshape(bsz * t, -1), *out_w)
    y_p = y_p.reshape(bsz, t, d)
    rows5 = lambda a: a.reshape(a.shape[0], N_KV, HEAD_DIM, a.shape[2]).transpose(0, 3, 1, 2)[None]
    p_states = (rows5(kc), rows5(vc), rows5(ks), rows5(vs), rows5(kw[:, :, -WINDOW:]), rows5(vw[:, :, -WINDOW:]),
                u3[:, -(CONV_K - 1):][None])

    (u_s, q_s, kc_t, vc_t, ks_t, vs_t, kw_t, vw_t, _, _, _, _, gates_s) = _inproj(
        x_sample.reshape(nb, d), pos_s, g_pre_mix[l], w_cat, cc)
    kc_s, vc_s, ks_s, vs_s, kw_s, vw_s = (a[0].T for a in (kc_t, vc_t, ks_t, vs_t, kw_t, vw_t))
    cy_s = _conv_sample(cache_conv[l], u_s, conv_w[l], conv_b[l], conv_ln_g[l], conv_ln_b[l])
    feat = lambda a: a[l].transpose(0, 2, 3, 1).reshape(a.shape[1], KV_W, a.shape[2])
    kvc = _cmp_sample(page_table, feat(cache_k_cmp), feat(cache_v_cmp), wk, wv)
    q8 = q_s.astype(F32).reshape(nb, GQA_R, N_KV, HEAD_DIM).transpose(0, 2, 1, 3).reshape(nb, N_HEADS, HEAD_DIM)
    zq = jnp.zeros((nb, GQA_R, HEAD_DIM), F32)
    q_pad = jnp.concatenate([jnp.concatenate([q8[:, :GQA_R], zq], -1), jnp.concatenate([zq, q8[:, GQA_R:]], -1)], 1)
    o_c, picks = _sample_select(q_pad, kvc, past // SEL_BLOCK)
    idx = picks[:, :N_SEL]
    o_s = _sample_attend(idx, page_table, q_pad, gates_s, o_c, ks_s, vs_s, kw_s, vw_s, feat(cache_k_win),
                         feat(cache_v_win), feat(cache_k_sel), feat(cache_v_sel))
    out_w_s = (w_out_c, w_out[l][cc:].astype(_MXU_DTYPE)) + out_w[2:]
    y_s = _layer_out(x_sample.reshape(nb, d), cy_s.astype(_MXU_DTYPE), o_s.reshape(nb, -1).astype(_MXU_DTYPE),
                     *out_w_s)
    y_s = y_s.reshape(nb, 1, d)
    new_win = lambda buf, new: jnp.concatenate([buf[l], new.reshape(nb, 1, N_KV, HEAD_DIM)], 1)[:, -wb:][None]
    s_conv = jnp.concatenate([cache_conv[l], u_s[:, None, :]], 1)[:, -(CONV_K - 1):][None]
    s_states = (kv5(kc_s, nb, 1), kv5(vc_s, nb, 1), kv5(ks_s, nb, 1), kv5(vs_s, nb, 1),
                new_win(cache_k_win, kw_s), new_win(cache_v_win, vw_s), s_conv)
    return (y_p, y_s) + p_states + s_states
```

```python
import functools

import numpy as np
import jax
import jax.numpy as jnp
from jax import lax
from jax.experimental import pallas as pl
from jax.experimental.pallas import tpu as pltpu

HEAD_DIM = 64
N_KV = 2
GQA_R = 4
N_HEADS = N_KV * GQA_R
CONV_K = 31
ROT_DIM = 16
ROPE_THETA = 500000.0
CMP_LEN = 32
CMP_STRIDE = 16
CMP_HIDDEN = 2 * HEAD_DIM
SEL_BLOCK = 64
N_SEL = 16
WINDOW = 512
PAGE_SIZE = 128
EPS = 1e-6
NEG = -1e30
SEL_OFF = -(2.0 ** 100)
LOG2E = 1.4426950408889634
LANES = 128
KV_W = N_KV * HEAD_DIM
N_GATE = 3 * N_HEADS
HALO = 32
SEL_CHUNK = 512
VMEM_LIMIT = 56 * 1024 * 1024

_MXU_DTYPE = jnp.bfloat16
F32 = jnp.float32


def _dot(a, b):
    return jnp.dot(a, b, preferred_element_type=F32)


def _dot_nt(a, b):
    return lax.dot_general(a, b, (((1,), (1,)), ((), ())), preferred_element_type=F32)


def _rms(x, g):
    return x * lax.rsqrt(jnp.mean(x * x, -1, keepdims=True) + EPS) * g


def _const_spec(shape):
    n = len(shape)
    return pl.BlockSpec(shape, lambda *_: (0,) * n, pipeline_mode=pl.Buffered(1))


def _params(sem):
    return pltpu.CompilerParams(dimension_semantics=sem, vmem_limit_bytes=VMEM_LIMIT)


def _inproj_kernel(x_ref, g_ref, w_ref, cos_ref, sa_ref, sb_ref, oh_ref,
                   u_ref, q_ref, kc_ref, vc_ref, ks_ref, vs_ref, kw_ref, vw_ref,
                   ksa_ref, vsb_ref, kwb_ref, vwb_ref, gate_ref):
    cc = u_ref.shape[1]
    xb = _rms(x_ref[...], g_ref[...]).astype(_MXU_DTYPE)
    cos, sa, sb = cos_ref[...], sa_ref[...], sb_ref[...]

    def seg(lo, hi):
        return _dot(xb, w_ref[:, lo:hi])

    def rope(v):
        return v * cos + pltpu.roll(v, LANES - ROT_DIM // 2, 1) * sa + pltpu.roll(v, ROT_DIM // 2, 1) * sb

    z = seg(0, 2 * cc)
    u_ref[...] = z[:, :cc] * jax.nn.sigmoid(z[:, cc:])
    c0 = 2 * cc
    zq = seg(c0, c0 + N_HEADS * HEAD_DIM)
    for c in range(GQA_R):
        q_ref[:, c * LANES:(c + 1) * LANES] = rope(zq[:, c * LANES:(c + 1) * LANES]).astype(q_ref.dtype)
    c1 = c0 + N_HEADS * HEAD_DIM
    zkv = seg(c1, c1 + 6 * KV_W)
    kc = rope(zkv[:, 0 * KV_W:1 * KV_W])
    vc = zkv[:, 1 * KV_W:2 * KV_W]
    ks = rope(zkv[:, 2 * KV_W:3 * KV_W])
    vs = zkv[:, 3 * KV_W:4 * KV_W]
    kw = rope(zkv[:, 4 * KV_W:5 * KV_W])
    vw = zkv[:, 5 * KV_W:6 * KV_W]
    kc_ref[0] = kc.T
    vc_ref[0] = vc.T
    ks_ref[0] = ks.T
    vs_ref[0] = vs.T
    kw_ref[0] = kw.T
    vw_ref[0] = vw.T
    ksa_ref[:, :KV_W] = ks.astype(ksa_ref.dtype)
    ksa_ref[:, KV_W:] = oh_ref[...]
    vsb_ref[...] = vs.astype(vsb_ref.dtype)
    kwb_ref[...] = kw.astype(kwb_ref.dtype)
    vwb_ref[...] = vw.astype(vwb_ref.dtype)
    c2 = c1 + 6 * KV_W
    gate_ref[...] = jax.nn.sigmoid(seg(c2, c2 + LANES))


def _rope_tables(pos):
    half = ROT_DIM // 2
    inv = jnp.power(jnp.float32(ROPE_THETA), -jnp.arange(half, dtype=jnp.float32) * 2.0 / ROT_DIM)
    ang = pos.astype(jnp.float32)[:, None] * inv[None, :]
    cos, sin = jnp.cos(ang), jnp.sin(ang)
    n = pos.shape[0]
    ones = jnp.ones((n, HEAD_DIM - ROT_DIM), F32)
    zeros8 = jnp.zeros((n, half), F32)
    zeros_rest = jnp.zeros((n, HEAD_DIM - ROT_DIM), F32)
    cos_h = jnp.concatenate([cos, cos, ones], 1)
    sa_h = jnp.concatenate([-sin, zeros8, zeros_rest], 1)
    sb_h = jnp.concatenate([zeros8, sin, zeros_rest], 1)
    two = lambda a: jnp.concatenate([a, a], 1)
    return two(cos_h), two(sa_h), two(sb_h)


def _block_onehot(pos):
    j = jnp.arange(LANES)[None, :]
    return ((pos[:, None] // SEL_BLOCK) == j).astype(_MXU_DTYPE)


def _inproj(x2d, pos, g_pre, w_cat, cc):
    n, d = x2d.shape
    period = pos.shape[0]
    tile = 512 if (n % 512 == 0 and period % 512 == 0) else period
    assert n % tile == 0 and period % tile == 0
    tpp = period // tile
    cos, sa, sb = _rope_tables(pos)
    oh = _block_onehot(pos)
    tok = lambda w: pl.BlockSpec((tile, w), lambda i: (i, 0))
    tab = lambda w: pl.BlockSpec((tile, w), lambda i: (i % tpp, 0))
    f32o = lambda w: jax.ShapeDtypeStruct((n, w), F32)
    b16o = lambda w: jax.ShapeDtypeStruct((n, w), _MXU_DTYPE)
    cache_o = jax.ShapeDtypeStruct((n // period, KV_W, period), F32)
    cache_spec = pl.BlockSpec((1, KV_W, tile), lambda i: (i // tpp, 0, i % tpp))
    out_shape = (f32o(cc), b16o(N_HEADS * HEAD_DIM)) + (cache_o,) * 6 + (
        b16o(2 * KV_W), b16o(KV_W), b16o(KV_W), b16o(KV_W), f32o(LANES))
    out_specs = (tok(cc), tok(N_HEADS * HEAD_DIM)) + (cache_spec,) * 6 + (
        tok(2 * KV_W), tok(KV_W), tok(KV_W), tok(KV_W), tok(LANES))
    return pl.pallas_call(
        _inproj_kernel,
        grid=(n // tile,),
        in_specs=[tok(d), _const_spec((1, d)), _const_spec(w_cat.shape), tab(LANES), tab(LANES), tab(LANES),
                  tab(LANES)],
        out_specs=out_specs,
        out_shape=out_shape,
        compiler_params=_params(("parallel",)),
        name="inproj",
    )(x2d, g_pre.reshape(1, d), w_cat, cos, sa, sb, oh)


def _ln_silu(y, g, b):
    yc = y - jnp.mean(y, -1, keepdims=True)
    yn = yc * lax.rsqrt(jnp.mean(yc * yc, -1, keepdims=True) + EPS) * g + b
    return yn * jax.nn.sigmoid(yn)


def _conv_prompt_kernel(u_ref, halo_ref, w_ref, b_ref, g_ref, bb_ref, o_ref, ext_ref, *, rows):
    i = pl.program_id(1)
    tc = u_ref.shape[1]
    ext_ref[0:HALO, :] = jnp.where(i == 0, 0.0, halo_ref[0])
    ext_ref[HALO:, :] = u_ref[0]
    lead = HALO - (CONV_K - 1)

    for c in range(tc // rows):
        base = c * rows
        acc = jnp.zeros((rows, u_ref.shape[2]), F32)
        for r in range(8):
            taps = [k for k in range(CONV_K) if (lead + k) % 8 == r]
            span = max((lead + k) // 8 for k in taps) * 8 + rows
            win = ext_ref[base + r:base + r + span, :]
            part = None
            for k in taps:
                a = (lead + k) // 8 * 8
                term = win[a:a + rows] * w_ref[k:k + 1, :]
                part = term if part is None else part + term
            acc = acc + part
        y = _ln_silu(acc + b_ref[...], g_ref[...], bb_ref[...])
        o_ref[0, base:base + rows, :] = y.astype(o_ref.dtype)


def _conv_prompt(u, conv_w, conv_b, ln_g, ln_b):
    bsz, t, cc = u.shape
    tc = 512 if t % 512 == 0 else t
    rows = 64
    assert t % tc == 0 and tc % rows == 0 and tc % HALO == 0
    hb = tc // HALO
    vec = lambda a: a.reshape(1, cc)
    return pl.pallas_call(
        functools.partial(_conv_prompt_kernel, rows=rows),
        grid=(bsz, t // tc),
        in_specs=[pl.BlockSpec((1, tc, cc), lambda b, i: (b, i, 0)),
                  pl.BlockSpec((1, HALO, cc), lambda b, i: (b, jnp.maximum(i * hb - 1, 0), 0)),
                  _const_spec((CONV_K, cc)), _const_spec((1, cc)), _const_spec((1, cc)), _const_spec((1, cc))],
        out_specs=pl.BlockSpec((1, tc, cc), lambda b, i: (b, i, 0)),
        out_shape=jax.ShapeDtypeStruct((bsz, t, cc), _MXU_DTYPE),
        scratch_shapes=[pltpu.VMEM((tc + HALO, cc), F32)],
        compiler_params=_params(("parallel", "parallel")),
        name="conv_prompt",
    )(u, u, conv_w, vec(conv_b), vec(ln_g), vec(ln_b))


def _conv_sample_kernel(cache_ref, u_ref, w_ref, b_ref, g_ref, bb_ref, o_ref):
    nb = cache_ref.shape[0]
    w_hist = w_ref[0:CONV_K - 1, :]
    rows = [jnp.sum(cache_ref[i] * w_hist, axis=0, keepdims=True) for i in range(nb)]
    y = jnp.concatenate(rows, 0) + u_ref[...] * w_ref[CONV_K - 1:CONV_K, :] + b_ref[...]
    o_ref[...] = _ln_silu(y, g_ref[...], bb_ref[...])


def _conv_sample(cache_conv, u, conv_w, conv_b, ln_g, ln_b):
    nb, hist, cc = cache_conv.shape
    blk = 8
    assert nb % blk == 0 and hist == CONV_K - 1
    vec = lambda a: a.reshape(1, cc)
    return pl.pallas_call(
        _conv_sample_kernel,
        grid=(nb // blk,),
        in_specs=[pl.BlockSpec((blk, hist, cc), lambda i: (i, 0, 0)), pl.BlockSpec((blk, cc), lambda i: (i, 0)),
                  _const_spec((CONV_K, cc)), _const_spec((1, cc)), _const_spec((1, cc)), _const_spec((1, cc))],
        out_specs=pl.BlockSpec((blk, cc), lambda i: (i, 0)),
        out_shape=jax.ShapeDtypeStruct((nb, cc), F32),
        compiler_params=_params(("parallel",)),
        name="conv_sample",
    )(cache_conv, u, conv_w, vec(conv_b), vec(ln_g), vec(ln_b))


CMP_GROUP = 16


def _compress_pages(page, n_pages, rows_ref, acc_ref, wbig_ref, bias_ref, w2_ref):
    per_group = min(CMP_GROUP, n_pages)
    gch = per_group * PAGE_SIZE // CMP_STRIDE
    nch = n_pages * PAGE_SIZE // CMP_STRIDE
    for q in range(n_pages // per_group):
        for p in range(q * per_group, (q + 1) * per_group):
            rows_ref[p * PAGE_SIZE:(p + 1) * PAGE_SIZE, :] = page(p).T
        r0 = q * per_group * PAGE_SIZE
        acc = None
        for jj in range(CMP_STRIDE // 2):
            a = rows_ref[pl.ds(r0 + 2 * jj, gch, stride=CMP_STRIDE), :]
            b = rows_ref[pl.ds(r0 + 2 * jj + 1, gch, stride=CMP_STRIDE), :]
            d = _dot(jnp.concatenate([a, b], axis=1).astype(_MXU_DTYPE), wbig_ref[jj])
            acc = d if acc is None else acc + d
        acc_ref[q * gch:(q + 1) * gch, :] = acc
    hw = N_KV * CMP_HIDDEN
    first, second = acc_ref[:, :hw], acc_ref[:, hw:]
    second_next = pltpu.roll(second, nch - 1, 0)
    h = jax.nn.gelu(first + second_next + bias_ref[...], approximate=True)
    out = _dot(h.astype(_MXU_DTYPE), w2_ref[...])
    row = lax.broadcasted_iota(jnp.int32, out.shape, 0)
    return jnp.where(row < nch - 1, out, 0.0)


def _compress_weights(w1, b1, w2, pe):
    eye = jnp.eye(N_KV, dtype=F32)
    w1s = w1.reshape(2, CMP_STRIDE // 2, 2, HEAD_DIM, CMP_HIDDEN)
    wbig = jnp.einsum('sjldh,ge->jlgdseh', w1s, eye)
    wbig = wbig.reshape(CMP_STRIDE // 2, 2 * KV_W, 2 * N_KV * CMP_HIDDEN).astype(_MXU_DTYPE)
    bias = b1 + jnp.einsum('jd,jdh->h', pe, w1, precision=lax.Precision.HIGHEST)
    bias2 = jnp.tile(bias, N_KV).reshape(1, N_KV * CMP_HIDDEN)
    w2big = jnp.einsum('hd,ge->ghed', w2, eye).reshape(N_KV * CMP_HIDDEN, KV_W).astype(_MXU_DTYPE)
    return wbig, bias2, w2big


def _cmp_prompt_kernel(rk_ref, rv_ref, wk_ref, bk_ref, w2k_ref, wv_ref, bv_ref, w2v_ref, kc_ref, vc_ref,
                       rows_ref, acc_ref):
    n_pages = rk_ref.shape[2] // PAGE_SIZE
    for src, w, bias, w2, dst in ((rk_ref, wk_ref, bk_ref, w2k_ref, kc_ref), (rv_ref, wv_ref, bv_ref, w2v_ref, vc_ref)):
        page = lambda p, src=src: src[0, :, p * PAGE_SIZE:(p + 1) * PAGE_SIZE]
        dst[0] = _compress_pages(page, n_pages, rows_ref, acc_ref, w, bias, w2).astype(dst.dtype)


def _cmp_prompt(rows_k, rows_v, wk, wv):
    bsz, _, t = rows_k.shape
    nch = t // CMP_STRIDE
    assert t % PAGE_SIZE == 0
    wspecs = [_const_spec(a.shape) for a in wk]
    row_spec = pl.BlockSpec((1, KV_W, t), lambda b: (b, 0, 0))
    out_spec = pl.BlockSpec((1, nch, KV_W), lambda b: (b, 0, 0))
    out = jax.ShapeDtypeStruct((bsz, nch, KV_W), _MXU_DTYPE)
    return pl.pallas_call(
        _cmp_prompt_kernel,
        grid=(bsz,),
        in_specs=[row_spec, row_spec] + wspecs + wspecs,
        out_specs=(out_spec, out_spec),
        out_shape=(out, out),
        scratch_shapes=[pltpu.VMEM((t, KV_W), F32), pltpu.VMEM((nch, 2 * N_KV * CMP_HIDDEN), F32)],
        compiler_params=_params(("parallel",)),
        name="cmp_prompt",
    )(rows_k, rows_v, *wk, *wv)


def _cmp_sample_kernel(pt_ref, pk_ref, pv_ref, w1_ref, b_ref, w2_ref, out_ref, buf, rows_ref, acc_ref, sem, *,
                       n_pages):
    b = pl.program_id(0)
    s = pl.program_id(1)

    def copies(pool_ref, seq, slot, wait):
        for p in range(n_pages):
            cp = pltpu.make_async_copy(pool_ref.at[0 if wait else pt_ref[seq, p]], buf.at[slot, p], sem.at[slot])
            if wait:
                cp.wait()
            else:
                cp.start()

    @pl.when((b == 0) & (s == 0))
    def _():
        copies(pk_ref, 0, 0, False)

    @pl.when(s == 0)
    def _():
        copies(pv_ref, b, 1, False)

    @pl.when((s == 1) & (b + 1 < pl.num_programs(0)))
    def _():
        copies(pk_ref, b + 1, 0, False)

    copies(pk_ref, b, s, True)

    page = lambda p: buf[s, p]
    out = _compress_pages(page, n_pages, rows_ref, acc_ref, w1_ref.at[0], b_ref.at[0], w2_ref.at[0])
    out_ref[0, 0] = out.astype(out_ref.dtype)


def _cmp_sample(page_table, pool_k, pool_v, wk, wv):
    nb, n_pages = page_table.shape
    past = n_pages * PAGE_SIZE
    nch = past // CMP_STRIDE
    w1, bias, w2 = (jnp.stack([a, c]) for a, c in zip(wk, wv))
    sel = lambda a: pl.BlockSpec((1,) + a.shape[1:], lambda b, s, pt, n=a.ndim: (s,) + (0,) * (n - 1))
    any_spec = pl.BlockSpec(memory_space=pl.ANY)
    return pl.pallas_call(
        functools.partial(_cmp_sample_kernel, n_pages=n_pages),
        grid_spec=pltpu.PrefetchScalarGridSpec(
            num_scalar_prefetch=1,
            grid=(nb, 2),
            in_specs=[any_spec, any_spec, sel(w1), sel(bias), sel(w2)],
            out_specs=pl.BlockSpec((1, 1, nch, KV_W), lambda b, s, pt: (s, b, 0, 0)),
            scratch_shapes=[pltpu.VMEM((2, n_pages, KV_W, PAGE_SIZE), F32), pltpu.VMEM((past, KV_W), F32),
                            pltpu.VMEM((nch, 2 * N_KV * CMP_HIDDEN), F32), pltpu.SemaphoreType.DMA((2,))]),
        out_shape=jax.ShapeDtypeStruct((2, nb, nch, KV_W), _MXU_DTYPE),
        compiler_params=_params(("arbitrary", "arbitrary")),
        name="cmp_sample",
    )(page_table, pool_k, pool_v, w1, bias, w2)


def _gate_expand_matrix():
    e = np.zeros((2 * LANES, 3 * GQA_R * LANES), np.float32)
    for br in range(3):
        for g in range(N_KV):
            for r in range(GQA_R):
                c = br * N_HEADS + g * GQA_R + r
                lo = (br * GQA_R + r) * LANES + g * HEAD_DIM
                e[c, lo:lo + HEAD_DIM] = 1.0
                e[LANES + c, lo:lo + HEAD_DIM] = 1.0
    return jnp.asarray(e, dtype=_MXU_DTYPE)


def _expand_gates(gt, e_ref):
    hi = gt.astype(_MXU_DTYPE)
    lo = (gt - hi.astype(F32)).astype(_MXU_DTYPE)
    return _dot(jnp.concatenate([hi, lo], axis=1), e_ref[...])


def _overlap_t(n_cmp_pad, n_blk_pad, n_cmp, n_blk):
    i = np.arange(n_cmp_pad)[None, :]
    j = np.arange(n_blk_pad)[:, None]
    lo = np.maximum(i * CMP_STRIDE, j * SEL_BLOCK)
    hi = np.minimum(i * CMP_STRIDE + CMP_LEN, (j + 1) * SEL_BLOCK)
    ov = (np.maximum(hi - lo, 0) // CMP_STRIDE).astype(np.float32)
    ov = ov * (i < n_cmp) * (j < n_blk)
    return ov


def _nsa_prompt_kernel(q_ref, gate_ref, kc_ref, vc_ref, ksa_ref, vs_ref, kw_ref, vw_ref, ovt_ref, e_ref, o_ref,
                       m_ref, l_ref, acc_ref, *, seq_len):
    tq = q_ref.shape[1]
    start = pl.program_id(1) * tq
    qt = q_ref[0]
    lane = lax.broadcasted_iota(jnp.int32, (tq, LANES), 1)
    first_half = lane < HEAD_DIM
    gx = _expand_gates(gate_ref[0], e_ref)
    kc = kc_ref[0]
    vc = vc_ref[0]
    n_sel_rows = SEL_BLOCK

    def tok(shape):
        return start + (lax.broadcasted_iota(jnp.int32, shape, 0) & (tq - 1))

    def rep(v, n):
        return jnp.concatenate([v] * n, axis=1) if n > 1 else v

    wk = WINDOW + tq
    ws = pl.multiple_of(jnp.clip(start - WINDOW, 0, seq_len - wk), tq)
    n_full = start // SEL_CHUNK

    zero = jnp.zeros_like(qt[:, :LANES])
    lhs = jnp.concatenate(
        [jnp.where(first_half if g == 0 else ~first_half, qt[:, r * LANES:(r + 1) * LANES], zero)
         for g in range(N_KV) for r in range(GQA_R)], axis=0)

    s = _dot_nt(lhs, kc)
    vis = lax.broadcasted_iota(jnp.int32, s.shape, 1) * CMP_STRIDE + (CMP_LEN - 1) <= tok(s.shape)
    sm = jnp.where(vis, s, NEG)
    e = jnp.where(vis, jnp.exp2(sm - jnp.max(sm, -1, keepdims=True)), 0.0)
    p = e * (1.0 / jnp.maximum(jnp.sum(e, -1, keepdims=True), 1e-30))
    o_cmp = _dot(p.astype(_MXU_DTYPE), vc)
    psum = jnp.concatenate(
        [(p[(4 * g) * tq:(4 * g + 1) * tq] + p[(4 * g + 1) * tq:(4 * g + 2) * tq]) + p[(4 * g + 2) * tq:(4 * g + 3) * tq]
         + p[(4 * g + 3) * tq:(4 * g + 4) * tq] for g in range(N_KV)], axis=0)
    imp = _dot_nt(ovt_ref[...], psum.astype(_MXU_DTYPE))[:n_sel_rows]

    j = lax.broadcasted_iota(jnp.int32, imp.shape, 0)
    cur = (start + (lax.broadcasted_iota(jnp.int32, imp.shape, 1) & (tq - 1))) // SEL_BLOCK
    future = j > cur
    forced = (j == 0) | (j == cur) | (j == cur - 1)
    score = jnp.where(future, -jnp.inf, jnp.where(forced, jnp.inf, imp))
    slabs = [score[8 * v:8 * v + 8] for v in range(n_sel_rows // 8)]
    ranks = [jnp.zeros(slabs[0].shape, jnp.int32) for _ in slabs]
    sub = lax.broadcasted_iota(jnp.int32, slabs[0].shape, 0)
    for i in range(n_sel_rows):
        si = score[i:i + 1, :]
        for v, sl in enumerate(slabs):
            if v < i // 8:
                inc = jnp.where(si > sl, 1, 0)
            elif v > i // 8:
                inc = jnp.where(si >= sl, 1, 0)
            else:
                inc = jnp.where(sub > i % 8, jnp.where(si >= sl, 1, 0), jnp.where(si > sl, 1, 0))
            ranks[v] = ranks[v] + inc
    rank = jnp.concatenate(ranks, axis=0)
    bias_t = jnp.where(future, SEL_OFF, jnp.where(rank < N_SEL, 0.0, SEL_OFF))
    bias_t = jnp.concatenate([bias_t, jnp.zeros((LANES - n_sel_rows, N_KV * tq), F32)], axis=0)
    bias = jnp.transpose(bias_t).astype(_MXU_DTYPE)
    bias_rows = jnp.concatenate([bias[g * tq:(g + 1) * tq] for g in range(N_KV) for _ in range(GQA_R)], axis=0)
    lhs2 = jnp.concatenate([lhs, bias_rows], axis=1)

    m_ref[...] = jnp.full(m_ref.shape, NEG, F32)
    l_ref[...] = jnp.zeros(l_ref.shape, F32)
    acc_ref[...] = jnp.zeros(acc_ref.shape, F32)

    def sel_step(c, carry, causal=False):
        k0 = pl.multiple_of(c * SEL_CHUNK, SEL_CHUNK)
        s = _dot_nt(lhs2, ksa_ref[0, pl.ds(k0, SEL_CHUNK), :])
        if causal:
            kpos = k0 + lax.broadcasted_iota(jnp.int32, s.shape, 1)
            s = jnp.where(kpos <= tok(s.shape), s, NEG)
        m_prev = m_ref[...]
        m_new = jnp.maximum(m_prev, jnp.max(s, -1, keepdims=True))
        alpha = jnp.exp2(m_prev - m_new)
        pe = jnp.exp2(s - rep(m_new, SEL_CHUNK // LANES))
        l_ref[...] = alpha * l_ref[...] + jnp.sum(pe, -1, keepdims=True)
        acc_ref[...] = alpha * acc_ref[...] + _dot(pe.astype(_MXU_DTYPE), vs_ref[0, pl.ds(k0, SEL_CHUNK), :])
        m_ref[...] = m_new
        return carry

    def two_steps(i, carry):
        sel_step(2 * i, 0)
        sel_step(2 * i + 1, 0)
        return carry

    lax.fori_loop(0, n_full // 2, two_steps, 0)
    lax.fori_loop(n_full - n_full % 2, n_full, sel_step, 0)
    sel_step(n_full, 0, causal=True)
    o_sel = acc_ref[...] / l_ref[...]

    s = _dot_nt(lhs, kw_ref[0, pl.ds(ws, wk), :])
    kpos = ws + lax.broadcasted_iota(jnp.int32, s.shape, 1)
    t = tok(s.shape)
    s = jnp.where(kpos <= t, jnp.where(kpos >= t - WINDOW, s, NEG), NEG)
    e = jnp.exp2(s - jnp.max(s, -1, keepdims=True))
    p = e * (1.0 / jnp.sum(e, -1, keepdims=True))
    o_win = _dot(p.astype(_MXU_DTYPE), vw_ref[0, pl.ds(ws, wk), :])

    for r in range(GQA_R):
        lo, hi = r * tq, (GQA_R + r) * tq
        out = jnp.zeros((tq, LANES), F32)
        for br, o_br in enumerate((o_cmp, o_sel, o_win)):
            both = jnp.where(first_half, o_br[lo:lo + tq], o_br[hi:hi + tq])
            col = (br * GQA_R + r) * LANES
            out = out + gx[:, col:col + LANES] * both
        o_ref[0, :, r * LANES:(r + 1) * LANES] = out.astype(o_ref.dtype)


def _nsa_prompt(q, gates, kc, vc, ksa, vsb, kwb, vwb):
    bsz, t, qw = q.shape
    tq = 128
    n_cmp = kc.shape[1]
    assert t % SEL_CHUNK == 0 and t >= WINDOW + tq and t // SEL_BLOCK <= SEL_BLOCK and qw == GQA_R * LANES
    ovt = jnp.asarray(_overlap_t(n_cmp, LANES, n_cmp - 1, t // SEL_BLOCK), dtype=_MXU_DTYPE)
    e = _gate_expand_matrix()
    tile = lambda w: pl.BlockSpec((1, tq, w), lambda b, i: (b, i, 0))
    full = lambda n, w: pl.BlockSpec((1, n, w), lambda b, i: (b, 0, 0))
    rows = N_HEADS * tq
    return pl.pallas_call(
        functools.partial(_nsa_prompt_kernel, seq_len=t),
        grid=(bsz, t // tq),
        in_specs=[tile(qw), tile(LANES), full(n_cmp, KV_W), full(n_cmp, KV_W), full(t, 2 * KV_W), full(t, KV_W),
                  full(t, KV_W), full(t, KV_W), _const_spec(ovt.shape), _const_spec(e.shape)],
        out_specs=tile(qw),
        out_shape=jax.ShapeDtypeStruct((bsz, t, qw), _MXU_DTYPE),
        scratch_shapes=[pltpu.VMEM((rows, LANES), F32)] * 3,
        compiler_params=_params(("parallel", "arbitrary")),
        name="nsa_prompt",
    )(q, gates, kc, vc, ksa, vsb, kwb, vwb, ovt, e)


SEQ_BLK = 8


def _sample_select_kernel(q_ref, kc_ref, vc_ref, ov_ref, oc_ref, idx_ref, *, n_vis, n_past_blk):
    psums = []
    for bi in range(SEQ_BLK):
        s = _dot_nt(q_ref[bi].astype(_MXU_DTYPE), kc_ref[0, bi])
        vis = lax.broadcasted_iota(jnp.int32, s.shape, 1) < n_vis
        sm = jnp.where(vis, s, NEG)
        e = jnp.where(vis, jnp.exp2(sm - jnp.max(sm, -1, keepdims=True)), 0.0)
        p = e * (1.0 / jnp.maximum(jnp.sum(e, -1, keepdims=True), 1e-30))
        oc_ref[bi] = _dot(p.astype(_MXU_DTYPE), vc_ref[0, bi])
        row = lax.broadcasted_iota(jnp.int32, p.shape, 0)
        psums.append(jnp.sum(jnp.where(row < GQA_R, p, 0.0), axis=0, keepdims=True))
        psums.append(jnp.sum(jnp.where(row >= GQA_R, p, 0.0), axis=0, keepdims=True))
    psum = jnp.concatenate(psums, axis=0)
    imp = _dot(psum.astype(_MXU_DTYPE), ov_ref[...])
    j = lax.broadcasted_iota(jnp.int32, imp.shape, 1)
    score = jnp.where(j >= n_past_blk, -jnp.inf,
                      jnp.where(j == 0, jnp.inf, jnp.where(j == n_past_blk - 1, jnp.inf, imp)))
    picks = jnp.zeros(imp.shape, jnp.int32)
    for it in range(N_SEL - 1):
        best = jnp.max(score, -1, keepdims=True)
        pick = jnp.min(jnp.where(score == best, j, LANES), -1, keepdims=True)
        picks = jnp.where(j == it, pick, picks)
        score = jnp.where(j == pick, -jnp.inf, score)
    idx_ref[...] = picks


def _sample_select(q_pad, kvc, n_past_blk):
    nb = q_pad.shape[0]
    n_cmp_pad = kvc.shape[2]
    assert n_past_blk <= LANES and n_past_blk >= N_SEL - 1 and nb % SEQ_BLK == 0
    ov = jnp.asarray(_overlap_t(n_cmp_pad, LANES, n_cmp_pad - 1, n_past_blk).T, dtype=_MXU_DTYPE)
    return pl.pallas_call(
        functools.partial(_sample_select_kernel, n_vis=n_cmp_pad - 1, n_past_blk=n_past_blk),
        grid=(nb // SEQ_BLK,),
        in_specs=[pl.BlockSpec((SEQ_BLK, N_HEADS, LANES), lambda i: (i, 0, 0)),
                  pl.BlockSpec((1, SEQ_BLK, n_cmp_pad, KV_W), lambda i: (0, i, 0, 0)),
                  pl.BlockSpec((1, SEQ_BLK, n_cmp_pad, KV_W), lambda i: (1, i, 0, 0)),
                  _const_spec(ov.shape)],
        out_specs=(pl.BlockSpec((SEQ_BLK, N_HEADS, KV_W), lambda i: (i, 0, 0)),
                   pl.BlockSpec((SEQ_BLK * N_KV, LANES), lambda i: (i, 0))),
        out_shape=(jax.ShapeDtypeStruct((nb, N_HEADS, KV_W), F32),
                   jax.ShapeDtypeStruct((nb * N_KV, LANES), jnp.int32)),
        compiler_params=_params(("parallel",)),
        name="sample_select",
    )(q_pad, kvc, kvc, ov)


BLK_PER_PAGE = PAGE_SIZE // SEL_BLOCK


def _block_copies(idx_ref, pt_ref, pool_ref, buf_ref, sem, seq, slot, wait):
    for g in range(N_KV):
        for i in range(N_SEL - 1):
            page = 0 if wait else pt_ref[seq, idx_ref[seq * N_KV + g, i] // BLK_PER_PAGE]
            cp = pltpu.make_async_copy(pool_ref.at[page], buf_ref.at[slot, g, i], sem)
            if wait:
                cp.wait()
            else:
                cp.start()


def _softmax_with_new(q, s, k_new):
    kn = k_new.astype(_MXU_DTYPE).astype(F32)
    s_new = jnp.sum(q.astype(F32) * kn, -1, keepdims=True)
    m = jnp.maximum(jnp.max(s, -1, keepdims=True), s_new)
    e = jnp.exp2(s - m)
    e_new = jnp.exp2(s_new - m)
    return e, e_new, jnp.sum(e, -1, keepdims=True) + e_new


def _sample_attend_kernel(idx_ref, pt_ref, q_ref, gate_ref, oc_ref, ksn_ref, vsn_ref, kwn_ref, vwn_ref,
                          kwin_ref, vwin_ref, pks_ref, pvs_ref, o_ref, kbuf, vbuf, sem):
    b = pl.program_id(0)
    slot = b % 2

    def gather(seq, sl, wait):
        _block_copies(idx_ref, pt_ref, pks_ref, kbuf, sem.at[0, sl], seq, sl, wait)
        _block_copies(idx_ref, pt_ref, pvs_ref, vbuf, sem.at[1, sl], seq, sl, wait)

    @pl.when(b == 0)
    def _():
        gather(0, 0, False)

    @pl.when(b + 1 < pl.num_programs(0))
    def _():
        gather(b + 1, 1 - slot, False)

    q = q_ref[0].astype(_MXU_DTYPE)
    top = lax.broadcasted_iota(jnp.int32, (N_HEADS, 1), 0) < GQA_R
    cast = lambda a: a.astype(_MXU_DTYPE)

    s = _dot(q, cast(kwin_ref[0]))
    e, e_new, den = _softmax_with_new(q, s, kwn_ref[0])
    o_w = (_dot_nt(cast(e), cast(vwin_ref[0])) + e_new * vwn_ref[0].astype(_MXU_DTYPE).astype(F32)) / den

    gather(b, slot, True)
    half = lax.broadcasted_iota(jnp.int32, (N_HEADS, PAGE_SIZE), 1) // SEL_BLOCK
    parts = []
    for i in range(N_SEL - 1):
        per_g = []
        for g in range(N_KV):
            want = idx_ref[b * N_KV + g, i] % BLK_PER_PAGE
            per_g.append(jnp.where(half == want, _dot(q, cast(kbuf[slot, g, i])), NEG))
        parts.append(jnp.where(top, per_g[0], per_g[1]))
    s = jnp.concatenate(parts, axis=1)
    e, e_new, den = _softmax_with_new(q, s, ksn_ref[0])
    pv = jnp.zeros((N_HEADS, KV_W), F32)
    for i in range(N_SEL - 1):
        ei = cast(e[:, i * PAGE_SIZE:(i + 1) * PAGE_SIZE])
        pv = pv + jnp.where(top, _dot_nt(ei, cast(vbuf[slot, 0, i])), _dot_nt(ei, cast(vbuf[slot, 1, i])))
    o_s = (pv + e_new * vsn_ref[0].astype(_MXU_DTYPE).astype(F32)) / den

    gates = jnp.broadcast_to(gate_ref[0], (N_HEADS, LANES))
    col = lax.broadcasted_iota(jnp.int32, gates.shape, 1) - lax.broadcasted_iota(jnp.int32, gates.shape, 0)
    out = jnp.zeros((N_HEADS, KV_W), F32)
    for br, o_br in enumerate((oc_ref[0], o_s, o_w)):
        out = out + jnp.sum(jnp.where(col == br * N_HEADS, gates, 0.0), -1, keepdims=True) * o_br
    o_ref[0] = jnp.where(top, out[:, :HEAD_DIM], out[:, HEAD_DIM:])


def _sample_attend(idx, page_table, q_pad, gates, o_c, ks_new, vs_new, kw_new, vw_new, buf_k_win, buf_v_win,
                   pool_k_sel, pool_v_sel):
    nb = q_pad.shape[0]
    wb = buf_k_win.shape[2]
    per = lambda n, w: pl.BlockSpec((1, n, w), lambda b, *_: (b, 0, 0))
    any_spec = pl.BlockSpec(memory_space=pl.ANY)
    new = lambda a: a.reshape(nb, 1, KV_W)
    pages = pltpu.VMEM((2, N_KV, N_SEL - 1, KV_W, PAGE_SIZE), F32)
    return pl.pallas_call(
        _sample_attend_kernel,
        grid_spec=pltpu.PrefetchScalarGridSpec(
            num_scalar_prefetch=2,
            grid=(nb,),
            in_specs=[per(N_HEADS, KV_W), per(1, LANES), per(N_HEADS, KV_W)] + [per(1, KV_W)] * 4 + [
                per(KV_W, wb), per(KV_W, wb), any_spec, any_spec],
            out_specs=per(N_HEADS, HEAD_DIM),
            scratch_shapes=[pages, pages, pltpu.SemaphoreType.DMA((2, 2))]),
        out_shape=jax.ShapeDtypeStruct((nb, N_HEADS, HEAD_DIM), F32),
        compiler_params=_params(("arbitrary",)),
        name="sample_attend",
    )(idx, page_table, q_pad, gates.reshape(nb, 1, LANES), o_c, new(ks_new), new(vs_new), new(kw_new), new(vw_new),
      buf_k_win, buf_v_win, pool_k_sel, pool_v_sel)


def _layer_out_kernel(x_ref, cy_ref, o_ref, woc_ref, woo_ref, gpost_ref, gpre2_ref, wup_ref, wdn_ref, gpost2_ref,
                      y_ref, *, ff_chunk):
    mix = _dot(cy_ref[...], woc_ref[...]) + _dot(o_ref[...], woo_ref[...])
    h = x_ref[...] + _rms(mix, gpost_ref[...])
    hn = _rms(h, gpre2_ref[...]).astype(_MXU_DTYPE)
    f = None
    for c in range(wup_ref.shape[1] // ff_chunk):
        a = jnp.maximum(_dot(hn, wup_ref[:, c * ff_chunk:(c + 1) * ff_chunk]), 0.0)
        d = _dot((a * a).astype(_MXU_DTYPE), wdn_ref[c * ff_chunk:(c + 1) * ff_chunk, :])
        f = d if f is None else f + d
    y_ref[...] = h + _rms(f, gpost2_ref[...])


def _layer_out(x2d, cy, o, w_out_c, w_out_o, g_post, g_pre2, w_up, w_down, g_post2):
    n, d = x2d.shape
    tile = 512 if n % 512 == 0 else n
    tok = lambda w: pl.BlockSpec((tile, w), lambda i: (i, 0))
    vec = lambda a: a.reshape(1, d)
    return pl.pallas_call(
        functools.partial(_layer_out_kernel, ff_chunk=1024),
        grid=(n // tile,),
        in_specs=[tok(d), tok(cy.shape[1]), tok(o.shape[1]), _const_spec(w_out_c.shape), _const_spec(w_out_o.shape),
                  _const_spec((1, d)), _const_spec((1, d)), _const_spec(w_up.shape), _const_spec(w_down.shape),
                  _const_spec((1, d))],
        out_specs=tok(d),
        out_shape=jax.ShapeDtypeStruct((n, d), F32),
        compiler_params=_params(("parallel",)),
        name="layer_out",
    )(x2d, cy, o, w_out_c, w_out_o, vec(g_post), vec(g_pre2), w_up, w_down, vec(g_post2))


def _prep_w_in(w_in, cc):
    d = w_in.shape[0]
    c0 = 2 * cc
    c1 = c0 + N_HEADS * HEAD_DIM
    c2 = c1 + 6 * KV_W
    wq = w_in[:, c0:c1].reshape(d, N_KV, GQA_R, HEAD_DIM).transpose(0, 2, 1, 3).reshape(d, N_HEADS * HEAD_DIM)
    wq = wq * (HEAD_DIM ** -0.5 * LOG2E)
    wg = jnp.pad(w_in[:, c2:], ((0, 0), (0, LANES - N_GATE)))
    return jnp.concatenate([w_in[:, :c0], wq, w_in[:, c1:c2], wg], axis=1).astype(_MXU_DTYPE)


def _prep_w_out(w_out, cc):
    d = w_out.shape[1]
    wo = w_out[cc:].reshape(N_KV, GQA_R, HEAD_DIM, d).transpose(1, 0, 2, 3).reshape(N_HEADS * HEAD_DIM, d)
    return w_out[:cc].astype(_MXU_DTYPE), wo.astype(_MXU_DTYPE)


def kernel(x_prompt, x_sample, cache_k_cmp, cache_v_cmp, cache_k_sel, cache_v_sel, cache_k_win, cache_v_win, cache_conv, page_table, g_pre_mix, w_in, conv_w, conv_b, conv_ln_g, conv_ln_b, cmp_k_w1, cmp_k_b1, cmp_k_w2, cmp_k_pe, cmp_v_w1, cmp_v_b1, cmp_v_w2, cmp_v_pe, w_out, g_post_mix, g_pre_ffn, w_up, w_down, g_post_ffn):
    bsz, t, d = x_prompt.shape
    nb, ts, _ = x_sample.shape
    depth = w_in.shape[0]
    cc = conv_w.shape[2]
    n_pages = page_table.shape[1]
    past = n_pages * PAGE_SIZE
    wb = cache_k_win.shape[2]
    assert ts == 1 and depth == 1 and KV_W == LANES and past % SEL_BLOCK == 0 and wb == min(WINDOW, past)
    assert min(WINDOW, t) == WINDOW
    l = 0
    pos_p = jnp.arange(t)
    pos_s = jnp.full((nb,), past, jnp.int32)

    w_cat = _prep_w_in(w_in[l], cc)
    w_out_c, w_out_o = _prep_w_out(w_out[l], cc)
    w_up_b = w_up[l].astype(_MXU_DTYPE)
    w_down_b = w_down[l].astype(_MXU_DTYPE)
    wk = _compress_weights(cmp_k_w1[l], cmp_k_b1[l], cmp_k_w2[l], cmp_k_pe[l])
    wv = _compress_weights(cmp_v_w1[l], cmp_v_b1[l], cmp_v_w2[l], cmp_v_pe[l])
    out_w = (w_out_c, w_out_o, g_post_mix[l], g_pre_ffn[l], w_up_b, w_down_b, g_post_ffn[l])
    kv5 = lambda a, n, m: a.reshape(1, n, m, N_KV, HEAD_DIM)

    (u, q, kc, vc, ks, vs, kw, vw, ksa, vsb, kwb, vwb, gates) = _inproj(
        x_prompt.reshape(bsz * t, d), pos_p, g_pre_mix[l], w_cat, cc)
    u3 = u.reshape(bsz, t, cc)
    cy = _conv_prompt(u3, conv_w[l], conv_b[l], conv_ln_g[l], conv_ln_b[l])
    b3 = lambda a: a.reshape(bsz, t, a.shape[-1])
    kcc, vcc = _cmp_prompt(kc, vc, wk, wv)
    o = _nsa_prompt(b3(q), b3(gates), kcc, vcc, b3(ksa), b3(vsb), b3(kwb), b3(vwb))
    y_p = _layer_out(x_prompt.reshape(bsz * t, d), cy.reshape(bsz * t, cc), o.reshape(bsz * t, -1), *out_w)
    y_p = y_p.reshape(bsz, t, d)
    rows5 = lambda a: a.reshape(a.shape[0], N_KV, HEAD_DIM, a.shape[2]).transpose(0, 3, 1, 2)[None]
    p_states = (rows5(kc), rows5(vc), rows5(ks), rows5(vs), rows5(kw[:, :, -WINDOW:]), rows5(vw[:, :, -WINDOW:]),
                u3[:, -(CONV_K - 1):][None])

    (u_s, q_s, kc_t, vc_t, ks_t, vs_t, kw_t, vw_t, _, _, _, _, gates_s) = _inproj(
        x_sample.reshape(nb, d), pos_s, g_pre_mix[l], w_cat, cc)
    kc_s, vc_s, ks_s, vs_s, kw_s, vw_s = (a[0].T for a in (kc_t, vc_t, ks_t, vs_t, kw_t, vw_t))
    cy_s = _conv_sample(cache_conv[l], u_s, conv_w[l], conv_b[l], conv_ln_g[l], conv_ln_b[l])
    feat = lambda a: a[l].transpose(0, 2, 3, 1).reshape(a.shape[1], KV_W, a.shape[2])
    kvc = _cmp_sample(page_table, feat(cache_k_cmp), feat(cache_v_cmp), wk, wv)
    q8 = q_s.astype(F32).reshape(nb, GQA_R, N_KV, HEAD_DIM).transpose(0, 2, 1, 3).reshape(nb, N_HEADS, HEAD_DIM)
    zq = jnp.zeros((nb, GQA_R, HEAD_DIM), F32)
    q_pad = jnp.concatenate([jnp.concatenate([q8[:, :GQA_R], zq], -1), jnp.concatenate([zq, q8[:, GQA_R:]], -1)], 1)
    o_c, picks = _sample_select(q_pad, kvc, past // SEL_BLOCK)
    idx = picks[:, :N_SEL]
    o_s = _sample_attend(idx, page_table, q_pad, gates_s, o_c, ks_s, vs_s, kw_s, vw_s, feat(cache_k_win),
                         feat(cache_v_win), feat(cache_k_sel), feat(cache_v_sel))
    out_w_s = (w_out_c, w_out[l][cc:].astype(_MXU_DTYPE)) + out_w[2:]
    y_s = _layer_out(x_sample.reshape(nb, d), cy_s.astype(_MXU_DTYPE), o_s.reshape(nb, -1).astype(_MXU_DTYPE),
                     *out_w_s)
    y_s = y_s.reshape(nb, 1, d)
    new_win = lambda buf, new: jnp.concatenate([buf[l], new.reshape(nb, 1, N_KV, HEAD_DIM)], 1)[:, -wb:][None]
    s_conv = jnp.concatenate([cache_conv[l], u_s[:, None, :]], 1)[:, -(CONV_K - 1):][None]
    s_states = (kv5(kc_s, nb, 1), kv5(vc_s, nb, 1), kv5(ks_s, nb, 1), kv5(vs_s, nb, 1),
                new_win(cache_k_win, kw_s), new_win(cache_v_win, vw_s), s_conv)
    return (y_p, y_s) + p_states + s_states
```

```python
import functools

import numpy as np
import jax
import jax.numpy as jnp
from jax import lax
from jax.experimental import pallas as pl
from jax.experimental.pallas import tpu as pltpu

HEAD_DIM = 64
N_KV = 2
GQA_R = 4
N_HEADS = N_KV * GQA_R
CONV_K = 31
ROT_DIM = 16
ROPE_THETA = 500000.0
CMP_LEN = 32
CMP_STRIDE = 16
CMP_HIDDEN = 2 * HEAD_DIM
SEL_BLOCK = 64
N_SEL = 16
WINDOW = 512
PAGE_SIZE = 128
EPS = 1e-6
NEG = -1e30
SEL_OFF = -(2.0 ** 100)
LOG2E = 1.4426950408889634
LANES = 128
KV_W = N_KV * HEAD_DIM
N_GATE = 3 * N_HEADS
HALO = 32
SEL_CHUNK = 512
VMEM_LIMIT = 56 * 1024 * 1024

_MXU_DTYPE = jnp.bfloat16
F32 = jnp.float32


def _dot(a, b):
    return jnp.dot(a, b, preferred_element_type=F32)


def _dot_nt(a, b):
    return lax.dot_general(a, b, (((1,), (1,)), ((), ())), preferred_element_type=F32)


def _rms(x, g):
    return x * lax.rsqrt(jnp.mean(x * x, -1, keepdims=True) + EPS) * g


def _const_spec(shape):
    n = len(shape)
    return pl.BlockSpec(shape, lambda *_: (0,) * n, pipeline_mode=pl.Buffered(1))


def _params(sem):
    return pltpu.CompilerParams(dimension_semantics=sem, vmem_limit_bytes=VMEM_LIMIT)


def _ln_silu(y, g, b):
    yc = y - jnp.mean(y, -1, keepdims=True)
    yn = yc * lax.rsqrt(jnp.mean(yc * yc, -1, keepdims=True) + EPS) * g + b
    return yn * jax.nn.sigmoid(yn)


def _conv_tile(ext_ref, w_ref, b_ref, g_ref, bb_ref, o_ref, rows):
    lead = HALO - (CONV_K - 1)
    for c in range(o_ref.shape[0] // rows):
        base = c * rows
        acc = jnp.zeros((rows, o_ref.shape[1]), F32)
        for r in range(8):
            taps = [k for k in range(CONV_K) if (lead + k) % 8 == r]
            span = max((lead + k) // 8 for k in taps) * 8 + rows
            win = ext_ref[base + r:base + r + span, :]
            part = None
            for k in taps:
                a = (lead + k) // 8 * 8
                term = win[a:a + rows] * w_ref[k:k + 1, :]
                part = term if part is None else part + term
            acc = acc + part
        y = _ln_silu(acc + b_ref[...], g_ref[...], bb_ref[...])
        o_ref[base:base + rows, :] = y.astype(o_ref.dtype)


def _inproj_kernel(*refs, fuse_conv, tiles_per_seq):
    x_ref, g_ref, w_ref, cos_ref, sa_ref, sb_ref, oh_ref = refs[:7]
    if fuse_conv:
        cw_ref, cb_ref, lg_ref, lb_ref, cy_ref, tail_ref = refs[7:13]
        ext_ref = refs[-1]
        outs = refs[13:-1]
        cc = cy_ref.shape[1]
    else:
        u_ref = refs[7]
        outs = refs[8:]
        cc = u_ref.shape[1]
    (q_ref, kc_ref, vc_ref, ks_ref, vs_ref, kw_ref, vw_ref, ksa_ref, vsb_ref, kwb_ref, vwb_ref, gate_ref) = outs
    xb = _rms(x_ref[...], g_ref[...]).astype(_MXU_DTYPE)
    cos, sa, sb = cos_ref[...], sa_ref[...], sb_ref[...]

    def seg(lo, hi):
        return _dot(xb, w_ref[:, lo:hi])

    def rope(v):
        return v * cos + pltpu.roll(v, LANES - ROT_DIM // 2, 1) * sa + pltpu.roll(v, ROT_DIM // 2, 1) * sb

    z = seg(0, 2 * cc)
    u = z[:, :cc] * jax.nn.sigmoid(z[:, cc:])
    if fuse_conv:
        tile = u.shape[0]
        first = pl.program_id(0) % tiles_per_seq == 0

        @pl.when(first)
        def _():
            ext_ref[0:HALO, :] = jnp.zeros((HALO, cc), F32)

        @pl.when(jnp.logical_not(first))
        def _():
            ext_ref[0:HALO, :] = ext_ref[tile:tile + HALO, :]

        ext_ref[HALO:, :] = u
        tail_ref[0] = u[tile - HALO:]
        _conv_tile(ext_ref, cw_ref, cb_ref, lg_ref, lb_ref, cy_ref, rows=64)
    else:
        u_ref[...] = u
    c0 = 2 * cc
    zq = seg(c0, c0 + N_HEADS * HEAD_DIM)
    for c in range(GQA_R):
        q_ref[0, c * LANES:(c + 1) * LANES, :] = rope(zq[:, c * LANES:(c + 1) * LANES]).T.astype(q_ref.dtype)
    c1 = c0 + N_HEADS * HEAD_DIM
    zkv = seg(c1, c1 + 6 * KV_W)
    kc = rope(zkv[:, 0 * KV_W:1 * KV_W])
    vc = zkv[:, 1 * KV_W:2 * KV_W]
    ks = rope(zkv[:, 2 * KV_W:3 * KV_W])
    vs = zkv[:, 3 * KV_W:4 * KV_W]
    kw = rope(zkv[:, 4 * KV_W:5 * KV_W])
    vw = zkv[:, 5 * KV_W:6 * KV_W]
    vs_t, vw_t = vs.T, vw.T
    kc_ref[0] = kc.T
    vc_ref[0] = vc.T
    ks_ref[0] = ks.T
    vs_ref[0] = vs_t
    kw_ref[0] = kw.T
    vw_ref[0] = vw_t
    ksa_ref[:, :KV_W] = ks.astype(ksa_ref.dtype)
    ksa_ref[:, KV_W:] = oh_ref[...]
    vsb_ref[0] = vs_t.astype(vsb_ref.dtype)
    kwb_ref[...] = kw.astype(kwb_ref.dtype)
    vwb_ref[0] = vw_t.astype(vwb_ref.dtype)
    c2 = c1 + 6 * KV_W
    gate_ref[...] = jax.nn.sigmoid(seg(c2, c2 + LANES))


def _rope_tables(pos):
    half = ROT_DIM // 2
    inv = jnp.power(jnp.float32(ROPE_THETA), -jnp.arange(half, dtype=jnp.float32) * 2.0 / ROT_DIM)
    ang = pos.astype(jnp.float32)[:, None] * inv[None, :]
    cos, sin = jnp.cos(ang), jnp.sin(ang)
    n = pos.shape[0]
    ones = jnp.ones((n, HEAD_DIM - ROT_DIM), F32)
    zeros8 = jnp.zeros((n, half), F32)
    zeros_rest = jnp.zeros((n, HEAD_DIM - ROT_DIM), F32)
    cos_h = jnp.concatenate([cos, cos, ones], 1)
    sa_h = jnp.concatenate([-sin, zeros8, zeros_rest], 1)
    sb_h = jnp.concatenate([zeros8, sin, zeros_rest], 1)
    two = lambda a: jnp.concatenate([a, a], 1)
    return two(cos_h), two(sa_h), two(sb_h)


def _block_onehot(pos):
    j = jnp.arange(LANES)[None, :]
    return ((pos[:, None] // SEL_BLOCK) == j).astype(_MXU_DTYPE)


def _inproj(x2d, pos, g_pre, w_cat, cc, conv=None):
    n, d = x2d.shape
    period = pos.shape[0]
    tile = 512 if (n % 512 == 0 and period % 512 == 0) else period
    assert n % tile == 0 and period % tile == 0
    tpp = period // tile
    cos, sa, sb = _rope_tables(pos)
    oh = _block_onehot(pos)
    tok = lambda w: pl.BlockSpec((tile, w), lambda i: (i, 0))
    tab = lambda w: pl.BlockSpec((tile, w), lambda i: (i % tpp, 0))
    f32o = lambda w: jax.ShapeDtypeStruct((n, w), F32)
    b16o = lambda w: jax.ShapeDtypeStruct((n, w), _MXU_DTYPE)
    cache_o = jax.ShapeDtypeStruct((n // period, KV_W, period), F32)
    cache_spec = pl.BlockSpec((1, KV_W, tile), lambda i: (i // tpp, 0, i % tpp))
    feat_o = lambda w: jax.ShapeDtypeStruct((n // period, w, period), _MXU_DTYPE)
    feat_spec = lambda w: pl.BlockSpec((1, w, tile), lambda i: (i // tpp, 0, i % tpp))
    out_shape = (feat_o(N_HEADS * HEAD_DIM),) + (cache_o,) * 6 + (
        b16o(2 * KV_W), feat_o(KV_W), b16o(KV_W), feat_o(KV_W), f32o(LANES))
    out_specs = (feat_spec(N_HEADS * HEAD_DIM),) + (cache_spec,) * 6 + (
        tok(2 * KV_W), feat_spec(KV_W), tok(KV_W), feat_spec(KV_W), tok(LANES))
    in_specs = [tok(d), _const_spec((1, d)), _const_spec(w_cat.shape), tab(LANES), tab(LANES), tab(LANES), tab(LANES)]
    args = (x2d, g_pre.reshape(1, d), w_cat, cos, sa, sb, oh)
    scratch = []
    if conv is not None:
        assert tile % 64 == 0 and tile >= HALO
        conv_w, conv_b, ln_g, ln_b = conv
        vec = lambda a: a.reshape(1, cc)
        in_specs += [_const_spec((CONV_K, cc)), _const_spec((1, cc)), _const_spec((1, cc)), _const_spec((1, cc))]
        args += (conv_w, vec(conv_b), vec(ln_g), vec(ln_b))
        out_shape = (b16o(cc), jax.ShapeDtypeStruct((n // period, HALO, cc), F32)) + out_shape
        out_specs = (tok(cc), pl.BlockSpec((1, HALO, cc), lambda i: (i // tpp, 0, 0))) + out_specs
        scratch = [pltpu.VMEM((tile + HALO, cc), F32)]
    else:
        out_shape = (f32o(cc),) + out_shape
        out_specs = (tok(cc),) + out_specs
    return pl.pallas_call(
        functools.partial(_inproj_kernel, fuse_conv=conv is not None, tiles_per_seq=tpp),
        grid=(n // tile,),
        in_specs=in_specs,
        out_specs=out_specs,
        out_shape=out_shape,
        scratch_shapes=scratch,
        compiler_params=_params(("arbitrary",)),
        name="inproj",
    )(*args)


def _conv_sample_kernel(cache_ref, u_ref, w_ref, b_ref, g_ref, bb_ref, o_ref):
    nb = cache_ref.shape[0]
    w_hist = w_ref[0:CONV_K - 1, :]
    rows = [jnp.sum(cache_ref[i] * w_hist, axis=0, keepdims=True) for i in range(nb)]
    y = jnp.concatenate(rows, 0) + u_ref[...] * w_ref[CONV_K - 1:CONV_K, :] + b_ref[...]
    o_ref[...] = _ln_silu(y, g_ref[...], bb_ref[...])


def _conv_sample(cache_conv, u, conv_w, conv_b, ln_g, ln_b):
    nb, hist, cc = cache_conv.shape
    blk = 8
    assert nb % blk == 0 and hist == CONV_K - 1
    vec = lambda a: a.reshape(1, cc)
    return pl.pallas_call(
        _conv_sample_kernel,
        grid=(nb // blk,),
        in_specs=[pl.BlockSpec((blk, hist, cc), lambda i: (i, 0, 0)), pl.BlockSpec((blk, cc), lambda i: (i, 0)),
                  _const_spec((CONV_K, cc)), _const_spec((1, cc)), _const_spec((1, cc)), _const_spec((1, cc))],
        out_specs=pl.BlockSpec((blk, cc), lambda i: (i, 0)),
        out_shape=jax.ShapeDtypeStruct((nb, cc), F32),
        compiler_params=_params(("parallel",)),
        name="conv_sample",
    )(cache_conv, u, conv_w, vec(conv_b), vec(ln_g), vec(ln_b))


CMP_GROUP = 16


def _compress_pages(page, n_pages, rows_ref, acc_ref, wbig_ref, bias_ref, w2_ref):
    per_group = min(CMP_GROUP, n_pages)
    gch = per_group * PAGE_SIZE // CMP_STRIDE
    nch = n_pages * PAGE_SIZE // CMP_STRIDE
    for q in range(n_pages // per_group):
        for p in range(q * per_group, (q + 1) * per_group):
            rows_ref[p * PAGE_SIZE:(p + 1) * PAGE_SIZE, :] = page(p).T
        r0 = q * per_group * PAGE_SIZE
        acc = None
        for jj in range(CMP_STRIDE // 2):
            a = rows_ref[pl.ds(r0 + 2 * jj, gch, stride=CMP_STRIDE), :]
            b = rows_ref[pl.ds(r0 + 2 * jj + 1, gch, stride=CMP_STRIDE), :]
            d = _dot(jnp.concatenate([a, b], axis=1).astype(_MXU_DTYPE), wbig_ref[jj])
            acc = d if acc is None else acc + d
        acc_ref[q * gch:(q + 1) * gch, :] = acc
    hw = N_KV * CMP_HIDDEN
    first, second = acc_ref[:, :hw], acc_ref[:, hw:]
    second_next = pltpu.roll(second, nch - 1, 0)
    h = jax.nn.gelu(first + second_next + bias_ref[...], approximate=True)
    out = _dot(h.astype(_MXU_DTYPE), w2_ref[...])
    row = lax.broadcasted_iota(jnp.int32, out.shape, 0)
    return jnp.where(row < nch - 1, out, 0.0)


def _compress_weights(w1, b1, w2, pe):
    eye = jnp.eye(N_KV, dtype=F32)
    w1s = w1.reshape(2, CMP_STRIDE // 2, 2, HEAD_DIM, CMP_HIDDEN)
    wbig = jnp.einsum('sjldh,ge->jlgdseh', w1s, eye)
    wbig = wbig.reshape(CMP_STRIDE // 2, 2 * KV_W, 2 * N_KV * CMP_HIDDEN).astype(_MXU_DTYPE)
    bias = b1 + jnp.einsum('jd,jdh->h', pe, w1, precision=lax.Precision.HIGHEST)
    bias2 = jnp.tile(bias, N_KV).reshape(1, N_KV * CMP_HIDDEN)
    w2big = jnp.einsum('hd,ge->ghed', w2, eye).reshape(N_KV * CMP_HIDDEN, KV_W).astype(_MXU_DTYPE)
    return wbig, bias2, w2big


def _cmp_prompt_kernel(rk_ref, rv_ref, wk_ref, bk_ref, w2k_ref, wv_ref, bv_ref, w2v_ref, kc_ref, vc_ref,
                       rows_ref, acc_ref):
    n_pages = rk_ref.shape[2] // PAGE_SIZE
    page_k = lambda p: rk_ref[0, :, p * PAGE_SIZE:(p + 1) * PAGE_SIZE]
    page_v = lambda p: rv_ref[0, :, p * PAGE_SIZE:(p + 1) * PAGE_SIZE]
    kc_ref[0] = _compress_pages(page_k, n_pages, rows_ref, acc_ref, wk_ref, bk_ref, w2k_ref).astype(kc_ref.dtype)
    vc_ref[0] = _compress_pages(page_v, n_pages, rows_ref, acc_ref, wv_ref, bv_ref, w2v_ref).T.astype(vc_ref.dtype)


def _cmp_prompt(rows_k, rows_v, wk, wv):
    bsz, _, t = rows_k.shape
    nch = t // CMP_STRIDE
    assert t % PAGE_SIZE == 0
    wspecs = [_const_spec(a.shape) for a in wk]
    row_spec = pl.BlockSpec((1, KV_W, t), lambda b: (b, 0, 0))
    return pl.pallas_call(
        _cmp_prompt_kernel,
        grid=(bsz,),
        in_specs=[row_spec, row_spec] + wspecs + wspecs,
        out_specs=(pl.BlockSpec((1, nch, KV_W), lambda b: (b, 0, 0)), pl.BlockSpec((1, KV_W, nch), lambda b: (b, 0, 0))),
        out_shape=(jax.ShapeDtypeStruct((bsz, nch, KV_W), _MXU_DTYPE),
                   jax.ShapeDtypeStruct((bsz, KV_W, nch), _MXU_DTYPE)),
        scratch_shapes=[pltpu.VMEM((t, KV_W), F32), pltpu.VMEM((nch, 2 * N_KV * CMP_HIDDEN), F32)],
        compiler_params=_params(("parallel",)),
        name="cmp_prompt",
    )(rows_k, rows_v, *wk, *wv)


def _cmp_sample_kernel(pt_ref, pk_ref, pv_ref, w1_ref, b_ref, w2_ref, out_ref, buf, rows_ref, acc_ref, sem, *,
                       n_pages):
    b = pl.program_id(0)
    s = pl.program_id(1)

    def copies(pool_ref, seq, slot, wait):
        for p in range(n_pages):
            cp = pltpu.make_async_copy(pool_ref.at[0 if wait else pt_ref[seq, p]], buf.at[slot, p], sem.at[slot])
            if wait:
                cp.wait()
            else:
                cp.start()

    @pl.when((b == 0) & (s == 0))
    def _():
        copies(pk_ref, 0, 0, False)

    @pl.when(s == 0)
    def _():
        copies(pv_ref, b, 1, False)

    @pl.when((s == 1) & (b + 1 < pl.num_programs(0)))
    def _():
        copies(pk_ref, b + 1, 0, False)

    copies(pk_ref, b, s, True)

    page = lambda p: buf[s, p]
    out = _compress_pages(page, n_pages, rows_ref, acc_ref, w1_ref.at[0], b_ref.at[0], w2_ref.at[0])
    out_ref[0, 0] = out.astype(out_ref.dtype)


def _cmp_sample(page_table, pool_k, pool_v, wk, wv):
    nb, n_pages = page_table.shape
    past = n_pages * PAGE_SIZE
    nch = past // CMP_STRIDE
    w1, bias, w2 = (jnp.stack([a, c]) for a, c in zip(wk, wv))
    sel = lambda a: pl.BlockSpec((1,) + a.shape[1:], lambda b, s, pt, n=a.ndim: (s,) + (0,) * (n - 1))
    any_spec = pl.BlockSpec(memory_space=pl.ANY)
    return pl.pallas_call(
        functools.partial(_cmp_sample_kernel, n_pages=n_pages),
        grid_spec=pltpu.PrefetchScalarGridSpec(
            num_scalar_prefetch=1,
            grid=(nb, 2),
            in_specs=[any_spec, any_spec, sel(w1), sel(bias), sel(w2)],
            out_specs=pl.BlockSpec((1, 1, nch, KV_W), lambda b, s, pt: (s, b, 0, 0)),
            scratch_shapes=[pltpu.VMEM((2, n_pages, KV_W, PAGE_SIZE), F32), pltpu.VMEM((past, KV_W), F32),
                            pltpu.VMEM((nch, 2 * N_KV * CMP_HIDDEN), F32), pltpu.SemaphoreType.DMA((2,))]),
        out_shape=jax.ShapeDtypeStruct((2, nb, nch, KV_W), _MXU_DTYPE),
        compiler_params=_params(("arbitrary", "arbitrary")),
        name="cmp_sample",
    )(page_table, pool_k, pool_v, w1, bias, w2)


def _overlap_t(n_cmp_pad, n_blk_pad, n_cmp, n_blk):
    i = np.arange(n_cmp_pad)[None, :]
    j = np.arange(n_blk_pad)[:, None]
    lo = np.maximum(i * CMP_STRIDE, j * SEL_BLOCK)
    hi = np.minimum(i * CMP_STRIDE + CMP_LEN, (j + 1) * SEL_BLOCK)
    ov = (np.maximum(hi - lo, 0) // CMP_STRIDE).astype(np.float32)
    ov = ov * (i < n_cmp) * (j < n_blk)
    return ov


def _nsa_prompt_kernel(qt_ref, gate_ref, kc_ref, vct_ref, ksa_ref, vst_ref, kw_ref, vwt_ref, ovt_ref, o_ref,
                       m_ref, l_ref, acc_ref, *, seq_len):
    tq = qt_ref.shape[2]
    start = pl.program_id(1) * tq
    qt = qt_ref[0]
    top_half = lax.broadcasted_iota(jnp.int32, (LANES, tq), 0) < HEAD_DIM
    n_sel_rows = SEL_BLOCK

    def tok(shape):
        return start + (lax.broadcasted_iota(jnp.int32, shape, 1) & (tq - 1))

    def keypos(first, shape):
        return first + lax.broadcasted_iota(jnp.int32, shape, 0)

    wk = WINDOW + tq
    ws = pl.multiple_of(jnp.clip(start - WINDOW, 0, seq_len - wk), tq)
    n_full = start // SEL_CHUNK

    zero = jnp.zeros_like(qt[:LANES])
    lhs_t = jnp.concatenate(
        [jnp.where(top_half if g == 0 else ~top_half, qt[r * LANES:(r + 1) * LANES], zero)
         for g in range(N_KV) for r in range(GQA_R)], axis=1)

    s = _dot(kc_ref[0], lhs_t)
    vis = keypos(0, s.shape) * CMP_STRIDE + (CMP_LEN - 1) <= tok(s.shape)
    sm = jnp.where(vis, s, NEG)
    e = jnp.where(vis, jnp.exp2(sm - jnp.max(sm, 0, keepdims=True)), 0.0)
    p = e * (1.0 / jnp.maximum(jnp.sum(e, 0, keepdims=True), 1e-30))
    o_cmp = _dot(vct_ref[0], p.astype(_MXU_DTYPE))
    psum = jnp.concatenate(
        [(p[:, (4 * g) * tq:(4 * g + 1) * tq] + p[:, (4 * g + 1) * tq:(4 * g + 2) * tq])
         + p[:, (4 * g + 2) * tq:(4 * g + 3) * tq] + p[:, (4 * g + 3) * tq:(4 * g + 4) * tq]
         for g in range(N_KV)], axis=1)
    imp = _dot(ovt_ref[...], psum.astype(_MXU_DTYPE))[:n_sel_rows]

    j = lax.broadcasted_iota(jnp.int32, imp.shape, 0)
    cur = tok(imp.shape) // SEL_BLOCK
    future = j > cur
    forced = (j == 0) | (j == cur) | (j == cur - 1)
    score = jnp.where(future, -jnp.inf, jnp.where(forced, jnp.inf, imp))
    slabs = [score[8 * v:8 * v + 8] for v in range(n_sel_rows // 8)]
    ranks = [jnp.zeros(slabs[0].shape, jnp.int32) for _ in slabs]
    sub = lax.broadcasted_iota(jnp.int32, slabs[0].shape, 0)
    for i in range(n_sel_rows):
        si = score[i:i + 1, :]
        for v, sl in enumerate(slabs):
            if v < i // 8:
                inc = jnp.where(si > sl, 1, 0)
            elif v > i // 8:
                inc = jnp.where(si >= sl, 1, 0)
            else:
                inc = jnp.where(sub > i % 8, jnp.where(si >= sl, 1, 0), jnp.where(si > sl, 1, 0))
            ranks[v] = ranks[v] + inc
    rank = jnp.concatenate(ranks, axis=0)
    bias = jnp.where(future, SEL_OFF, jnp.where(rank < N_SEL, 0.0, SEL_OFF))
    bias = jnp.concatenate([bias, jnp.zeros((LANES - n_sel_rows, N_KV * tq), F32)], axis=0).astype(_MXU_DTYPE)
    bias_cols = jnp.concatenate([bias[:, g * tq:(g + 1) * tq] for g in range(N_KV) for _ in range(GQA_R)], axis=1)
    lhs2_t = jnp.concatenate([lhs_t, bias_cols], axis=0)

    m_ref[...] = jnp.full(m_ref.shape, NEG, F32)
    l_ref[...] = jnp.zeros(l_ref.shape, F32)
    acc_ref[...] = jnp.zeros(acc_ref.shape, F32)

    def sel_step(c, carry, causal=False):
        k0 = pl.multiple_of(c * SEL_CHUNK, SEL_CHUNK)
        s = _dot(ksa_ref[0, pl.ds(k0, SEL_CHUNK), :], lhs2_t)
        if causal:
            s = jnp.where(keypos(k0, s.shape) <= tok(s.shape), s, NEG)
        m_prev = m_ref[...]
        m_new = jnp.maximum(m_prev, jnp.max(s, 0, keepdims=True))
        alpha = jnp.exp2(m_prev - m_new)
        pe = jnp.exp2(s - m_new[0:1])
        l_ref[...] = alpha * l_ref[...] + jnp.sum(pe, 0, keepdims=True)
        acc_ref[...] = alpha[0:1] * acc_ref[...] + _dot(vst_ref[0, :, pl.ds(k0, SEL_CHUNK)], pe.astype(_MXU_DTYPE))
        m_ref[...] = m_new
        return carry

    def two_steps(i, carry):
        sel_step(2 * i, 0)
        sel_step(2 * i + 1, 0)
        return carry

    lax.fori_loop(0, n_full // 2, two_steps, 0)
    lax.fori_loop(n_full - n_full % 2, n_full, sel_step, 0)
    sel_step(n_full, 0, causal=True)
    o_sel = acc_ref[...] / l_ref[0:1]

    s = _dot(kw_ref[0, pl.ds(ws, wk), :], lhs_t)
    kpos = keypos(ws, s.shape)
    t = tok(s.shape)
    s = jnp.where(kpos <= t, jnp.where(kpos >= t - WINDOW, s, NEG), NEG)
    e = jnp.exp2(s - jnp.max(s, 0, keepdims=True))
    p = e * (1.0 / jnp.sum(e, 0, keepdims=True))
    o_win = _dot(vwt_ref[0, :, pl.ds(ws, wk)], p.astype(_MXU_DTYPE))

    gates_t = gate_ref[0].T
    for r in range(GQA_R):
        lo, hi = r * tq, (GQA_R + r) * tq
        out_t = jnp.zeros((LANES, tq), F32)
        for br, o_br in enumerate((o_cmp, o_sel, o_win)):
            g_lo = gates_t[br * N_HEADS + r:br * N_HEADS + r + 1]
            g_hi = gates_t[br * N_HEADS + GQA_R + r:br * N_HEADS + GQA_R + r + 1]
            out_t = out_t + jnp.where(top_half, o_br[:, lo:lo + tq] * g_lo, o_br[:, hi:hi + tq] * g_hi)
        o_ref[0, :, r * LANES:(r + 1) * LANES] = out_t.T.astype(o_ref.dtype)


def _nsa_prompt(q_t, gates, kc, vc_t, ksa, vs_t, kwb, vw_t):
    bsz, qw, t = q_t.shape
    tq = 128
    n_cmp = kc.shape[1]
    assert t % SEL_CHUNK == 0 and t >= WINDOW + tq and t // SEL_BLOCK <= SEL_BLOCK and qw == GQA_R * LANES
    ovt = jnp.asarray(_overlap_t(n_cmp, LANES, n_cmp - 1, t // SEL_BLOCK), dtype=_MXU_DTYPE)
    tile = lambda w: pl.BlockSpec((1, tq, w), lambda b, i: (b, i, 0))
    full = lambda n, w: pl.BlockSpec((1, n, w), lambda b, i: (b, 0, 0))
    cols = N_HEADS * tq
    return pl.pallas_call(
        functools.partial(_nsa_prompt_kernel, seq_len=t),
        grid=(bsz, t // tq),
        in_specs=[pl.BlockSpec((1, qw, tq), lambda b, i: (b, 0, i)), tile(LANES), full(n_cmp, KV_W),
                  full(KV_W, n_cmp), full(t, 2 * KV_W), full(KV_W, t), full(t, KV_W), full(KV_W, t),
                  _const_spec(ovt.shape)],
        out_specs=tile(qw),
        out_shape=jax.ShapeDtypeStruct((bsz, t, qw), _MXU_DTYPE),
        scratch_shapes=[pltpu.VMEM((8, cols), F32), pltpu.VMEM((8, cols), F32), pltpu.VMEM((KV_W, cols), F32)],
        compiler_params=_params(("parallel", "arbitrary")),
        name="nsa_prompt",
    )(q_t, gates, kc, vc_t, ksa, vs_t, kwb, vw_t, ovt)


SEQ_BLK = 8


def _sample_select_kernel(q_ref, kc_ref, vc_ref, ov_ref, oc_ref, idx_ref, *, n_vis, n_past_blk):
    psums = []
    for bi in range(SEQ_BLK):
        s = _dot_nt(q_ref[bi].astype(_MXU_DTYPE), kc_ref[0, bi])
        vis = lax.broadcasted_iota(jnp.int32, s.shape, 1) < n_vis
        sm = jnp.where(vis, s, NEG)
        e = jnp.where(vis, jnp.exp2(sm - jnp.max(sm, -1, keepdims=True)), 0.0)
        p = e * (1.0 / jnp.maximum(jnp.sum(e, -1, keepdims=True), 1e-30))
        oc_ref[bi] = _dot(p.astype(_MXU_DTYPE), vc_ref[0, bi])
        row = lax.broadcasted_iota(jnp.int32, p.shape, 0)
        psums.append(jnp.sum(jnp.where(row < GQA_R, p, 0.0), axis=0, keepdims=True))
        psums.append(jnp.sum(jnp.where(row >= GQA_R, p, 0.0), axis=0, keepdims=True))
    psum = jnp.concatenate(psums, axis=0)
    imp = _dot(psum.astype(_MXU_DTYPE), ov_ref[...])
    j = lax.broadcasted_iota(jnp.int32, imp.shape, 1)
    score = jnp.where(j >= n_past_blk, -jnp.inf,
                      jnp.where(j == 0, jnp.inf, jnp.where(j == n_past_blk - 1, jnp.inf, imp)))
    picks = jnp.zeros(imp.shape, jnp.int32)
    for it in range(N_SEL - 1):
        best = jnp.max(score, -1, keepdims=True)
        pick = jnp.min(jnp.where(score == best, j, LANES), -1, keepdims=True)
        picks = jnp.where(j == it, pick, picks)
        score = jnp.where(j == pick, -jnp.inf, score)
    idx_ref[...] = picks


def _sample_select(q_pad, kvc, n_past_blk):
    nb = q_pad.shape[0]
    n_cmp_pad = kvc.shape[2]
    assert n_past_blk <= LANES and n_past_blk >= N_SEL - 1 and nb % SEQ_BLK == 0
    ov = jnp.asarray(_overlap_t(n_cmp_pad, LANES, n_cmp_pad - 1, n_past_blk).T, dtype=_MXU_DTYPE)
    return pl.pallas_call(
        functools.partial(_sample_select_kernel, n_vis=n_cmp_pad - 1, n_past_blk=n_past_blk),
        grid=(nb // SEQ_BLK,),
        in_specs=[pl.BlockSpec((SEQ_BLK, N_HEADS, LANES), lambda i: (i, 0, 0)),
                  pl.BlockSpec((1, SEQ_BLK, n_cmp_pad, KV_W), lambda i: (0, i, 0, 0)),
                  pl.BlockSpec((1, SEQ_BLK, n_cmp_pad, KV_W), lambda i: (1, i, 0, 0)),
                  _const_spec(ov.shape)],
        out_specs=(pl.BlockSpec((SEQ_BLK, N_HEADS, KV_W), lambda i: (i, 0, 0)),
                   pl.BlockSpec((SEQ_BLK * N_KV, LANES), lambda i: (i, 0))),
        out_shape=(jax.ShapeDtypeStruct((nb, N_HEADS, KV_W), F32),
                   jax.ShapeDtypeStruct((nb * N_KV, LANES), jnp.int32)),
        compiler_params=_params(("parallel",)),
        name="sample_select",
    )(q_pad, kvc, kvc, ov)


BLK_PER_PAGE = PAGE_SIZE // SEL_BLOCK


def _block_copies(idx_ref, pt_ref, pool_ref, buf_ref, sem, seq, slot, wait):
    for g in range(N_KV):
        for i in range(N_SEL - 1):
            page = 0 if wait else pt_ref[seq, idx_ref[seq * N_KV + g, i] // BLK_PER_PAGE]
            cp = pltpu.make_async_copy(pool_ref.at[page], buf_ref.at[slot, g, i], sem)
            if wait:
                cp.wait()
            else:
                cp.start()


def _softmax_with_new(q, s, k_new):
    kn = k_new.astype(_MXU_DTYPE).astype(F32)
    s_new = jnp.sum(q.astype(F32) * kn, -1, keepdims=True)
    m = jnp.maximum(jnp.max(s, -1, keepdims=True), s_new)
    e = jnp.exp2(s - m)
    e_new = jnp.exp2(s_new - m)
    return e, e_new, jnp.sum(e, -1, keepdims=True) + e_new


def _sample_attend_kernel(idx_ref, pt_ref, q_ref, gate_ref, oc_ref, ksn_ref, vsn_ref, kwn_ref, vwn_ref,
                          kwin_ref, vwin_ref, pks_ref, pvs_ref, o_ref, kbuf, vbuf, sem):
    b = pl.program_id(0)
    slot = b % 2

    def gather(seq, sl, wait):
        _block_copies(idx_ref, pt_ref, pks_ref, kbuf, sem.at[0, sl], seq, sl, wait)
        _block_copies(idx_ref, pt_ref, pvs_ref, vbuf, sem.at[1, sl], seq, sl, wait)

    @pl.when(b == 0)
    def _():
        gather(0, 0, False)

    @pl.when(b + 1 < pl.num_programs(0))
    def _():
        gather(b + 1, 1 - slot, False)

    q = q_ref[0].astype(_MXU_DTYPE)
    top = lax.broadcasted_iota(jnp.int32, (N_HEADS, 1), 0) < GQA_R
    cast = lambda a: a.astype(_MXU_DTYPE)

    s = _dot(q, cast(kwin_ref[0]))
    e, e_new, den = _softmax_with_new(q, s, kwn_ref[0])
    o_w = (_dot_nt(cast(e), cast(vwin_ref[0])) + e_new * vwn_ref[0].astype(_MXU_DTYPE).astype(F32)) / den

    gather(b, slot, True)
    half = lax.broadcasted_iota(jnp.int32, (N_HEADS, PAGE_SIZE), 1) // SEL_BLOCK
    parts = []
    for i in range(N_SEL - 1):
        per_g = []
        for g in range(N_KV):
            want = idx_ref[b * N_KV + g, i] % BLK_PER_PAGE
            per_g.append(jnp.where(half == want, _dot(q, cast(kbuf[slot, g, i])), NEG))
        parts.append(jnp.where(top, per_g[0], per_g[1]))
    s = jnp.concatenate(parts, axis=1)
    e, e_new, den = _softmax_with_new(q, s, ksn_ref[0])
    pv = jnp.zeros((N_HEADS, KV_W), F32)
    for i in range(N_SEL - 1):
        ei = cast(e[:, i * PAGE_SIZE:(i + 1) * PAGE_SIZE])
        pv = pv + jnp.where(top, _dot_nt(ei, cast(vbuf[slot, 0, i])), _dot_nt(ei, cast(vbuf[slot, 1, i])))
    o_s = (pv + e_new * vsn_ref[0].astype(_MXU_DTYPE).astype(F32)) / den

    gates = jnp.broadcast_to(gate_ref[0], (N_HEADS, LANES))
    col = lax.broadcasted_iota(jnp.int32, gates.shape, 1) - lax.broadcasted_iota(jnp.int32, gates.shape, 0)
    out = jnp.zeros((N_HEADS, KV_W), F32)
    for br, o_br in enumerate((oc_ref[0], o_s, o_w)):
        out = out + jnp.sum(jnp.where(col == br * N_HEADS, gates, 0.0), -1, keepdims=True) * o_br
    o_ref[0] = jnp.where(top, out[:, :HEAD_DIM], out[:, HEAD_DIM:])


def _sample_attend(idx, page_table, q_pad, gates, o_c, ks_new, vs_new, kw_new, vw_new, buf_k_win, buf_v_win,
                   pool_k_sel, pool_v_sel):
    nb = q_pad.shape[0]
    wb = buf_k_win.shape[2]
    per = lambda n, w: pl.BlockSpec((1, n, w), lambda b, *_: (b, 0, 0))
    any_spec = pl.BlockSpec(memory_space=pl.ANY)
    new = lambda a: a.reshape(nb, 1, KV_W)
    pages = pltpu.VMEM((2, N_KV, N_SEL - 1, KV_W, PAGE_SIZE), F32)
    return pl.pallas_call(
        _sample_attend_kernel,
        grid_spec=pltpu.PrefetchScalarGridSpec(
            num_scalar_prefetch=2,
            grid=(nb,),
            in_specs=[per(N_HEADS, KV_W), per(1, LANES), per(N_HEADS, KV_W)] + [per(1, KV_W)] * 4 + [
                per(KV_W, wb), per(KV_W, wb), any_spec, any_spec],
            out_specs=per(N_HEADS, HEAD_DIM),
            scratch_shapes=[pages, pages, pltpu.SemaphoreType.DMA((2, 2))]),
        out_shape=jax.ShapeDtypeStruct((nb, N_HEADS, HEAD_DIM), F32),
        compiler_params=_params(("arbitrary",)),
        name="sample_attend",
    )(idx, page_table, q_pad, gates.reshape(nb, 1, LANES), o_c, new(ks_new), new(vs_new), new(kw_new), new(vw_new),
      buf_k_win, buf_v_win, pool_k_sel, pool_v_sel)


def _layer_out_kernel(x_ref, cy_ref, o_ref, woc_ref, woo_ref, gpost_ref, gpre2_ref, wup_ref, wdn_ref, gpost2_ref,
                      y_ref, *, ff_chunk):
    mix = _dot(cy_ref[...], woc_ref[...]) + _dot(o_ref[...], woo_ref[...])
    h = x_ref[...] + _rms(mix, gpost_ref[...])
    hn = _rms(h, gpre2_ref[...]).astype(_MXU_DTYPE)
    f = None
    for c in range(wup_ref.shape[1] // ff_chunk):
        a = jnp.maximum(_dot(hn, wup_ref[:, c * ff_chunk:(c + 1) * ff_chunk]), 0.0)
        d = _dot((a * a).astype(_MXU_DTYPE), wdn_ref[c * ff_chunk:(c + 1) * ff_chunk, :])
        f = d if f is None else f + d
    y_ref[...] = h + _rms(f, gpost2_ref[...])


def _layer_out(x2d, cy, o, w_out_c, w_out_o, g_post, g_pre2, w_up, w_down, g_post2):
    n, d = x2d.shape
    tile = 512 if n % 512 == 0 else n
    tok = lambda w: pl.BlockSpec((tile, w), lambda i: (i, 0))
    vec = lambda a: a.reshape(1, d)
    return pl.pallas_call(
        functools.partial(_layer_out_kernel, ff_chunk=1024),
        grid=(n // tile,),
        in_specs=[tok(d), tok(cy.shape[1]), tok(o.shape[1]), _const_spec(w_out_c.shape), _const_spec(w_out_o.shape),
                  _const_spec((1, d)), _const_spec((1, d)), _const_spec(w_up.shape), _const_spec(w_down.shape),
                  _const_spec((1, d))],
        out_specs=tok(d),
        out_shape=jax.ShapeDtypeStruct((n, d), F32),
        compiler_params=_params(("parallel",)),
        name="layer_out",
    )(x2d, cy, o, w_out_c, w_out_o, vec(g_post), vec(g_pre2), w_up, w_down, vec(g_post2))


def _prep_w_in(w_in, cc):
    d = w_in.shape[0]
    c0 = 2 * cc
    c1 = c0 + N_HEADS * HEAD_DIM
    c2 = c1 + 6 * KV_W
    wq = w_in[:, c0:c1].reshape(d, N_KV, GQA_R, HEAD_DIM).transpose(0, 2, 1, 3).reshape(d, N_HEADS * HEAD_DIM)
    wq = wq * (HEAD_DIM ** -0.5 * LOG2E)
    wg = jnp.pad(w_in[:, c2:], ((0, 0), (0, LANES - N_GATE)))
    return jnp.concatenate([w_in[:, :c0], wq, w_in[:, c1:c2], wg], axis=1).astype(_MXU_DTYPE)


def _prep_w_out(w_out, cc):
    d = w_out.shape[1]
    wo = w_out[cc:].reshape(N_KV, GQA_R, HEAD_DIM, d).transpose(1, 0, 2, 3).reshape(N_HEADS * HEAD_DIM, d)
    return w_out[:cc].astype(_MXU_DTYPE), wo.astype(_MXU_DTYPE)


def kernel(x_prompt, x_sample, cache_k_cmp, cache_v_cmp, cache_k_sel, cache_v_sel, cache_k_win, cache_v_win, cache_conv, page_table, g_pre_mix, w_in, conv_w, conv_b, conv_ln_g, conv_ln_b, cmp_k_w1, cmp_k_b1, cmp_k_w2, cmp_k_pe, cmp_v_w1, cmp_v_b1, cmp_v_w2, cmp_v_pe, w_out, g_post_mix, g_pre_ffn, w_up, w_down, g_post_ffn):
    bsz, t, d = x_prompt.shape
    nb, ts, _ = x_sample.shape
    depth = w_in.shape[0]
    cc = conv_w.shape[2]
    n_pages = page_table.shape[1]
    past = n_pages * PAGE_SIZE
    wb = cache_k_win.shape[2]
    assert ts == 1 and depth == 1 and KV_W == LANES and past % SEL_BLOCK == 0 and wb == min(WINDOW, past)
    assert min(WINDOW, t) == WINDOW
    l = 0
    pos_p = jnp.arange(t)
    pos_s = jnp.full((nb,), past, jnp.int32)

    w_cat = _prep_w_in(w_in[l], cc)
    w_out_c, w_out_o = _prep_w_out(w_out[l], cc)
    w_up_b = w_up[l].astype(_MXU_DTYPE)
    w_down_b = w_down[l].astype(_MXU_DTYPE)
    wk = _compress_weights(cmp_k_w1[l], cmp_k_b1[l], cmp_k_w2[l], cmp_k_pe[l])
    wv = _compress_weights(cmp_v_w1[l], cmp_v_b1[l], cmp_v_w2[l], cmp_v_pe[l])
    out_w = (w_out_c, w_out_o, g_post_mix[l], g_pre_ffn[l], w_up_b, w_down_b, g_post_ffn[l])
    kv5 = lambda a, n, m: a.reshape(1, n, m, N_KV, HEAD_DIM)

    (cy, u_tail, q_t, kc, vc, ks, vs, kw, vw, ksa, vs_t, kwb, vw_t, gates) = _inproj(
        x_prompt.reshape(bsz * t, d), pos_p, g_pre_mix[l], w_cat, cc,
        conv=(conv_w[l], conv_b[l], conv_ln_g[l], conv_ln_b[l]))
    b3 = lambda a: a.reshape(bsz, t, a.shape[-1])
    kcc, vcc_t = _cmp_prompt(kc, vc, wk, wv)
    o = _nsa_prompt(q_t, b3(gates), kcc, vcc_t, b3(ksa), vs_t, b3(kwb), vw_t)
    y_p = _layer_out(x_prompt.reshape(bsz * t, d), cy.reshape(bsz * t, cc), o.reshape(bsz * t, -1), *out_w)
    y_p = y_p.reshape(bsz, t, d)
    rows5 = lambda a: a.reshape(a.shape[0], N_KV, HEAD_DIM, a.shape[2]).transpose(0, 3, 1, 2)[None]
    p_states = (rows5(kc), rows5(vc), rows5(ks), rows5(vs), rows5(kw[:, :, -WINDOW:]), rows5(vw[:, :, -WINDOW:]),
                u_tail[:, -(CONV_K - 1):][None])

    (u_s, q_st, kc_st, vc_st, ks_st, vs_st, kw_st, vw_st, _, _, _, _, gates_s) = _inproj(
        x_sample.reshape(nb, d), pos_s, g_pre_mix[l], w_cat, cc)
    kc_s, vc_s, ks_s, vs_s, kw_s, vw_s = (a[0].T for a in (kc_st, vc_st, ks_st, vs_st, kw_st, vw_st))
    q_s = q_st[0].T
    cy_s = _conv_sample(cache_conv[l], u_s, conv_w[l], conv_b[l], conv_ln_g[l], conv_ln_b[l])
    feat = lambda a: a[l].transpose(0, 2, 3, 1).reshape(a.shape[1], KV_W, a.shape[2])
    kvc = _cmp_sample(page_table, feat(cache_k_cmp), feat(cache_v_cmp), wk, wv)
    q8 = q_s.astype(F32).reshape(nb, GQA_R, N_KV, HEAD_DIM).transpose(0, 2, 1, 3).reshape(nb, N_HEADS, HEAD_DIM)
    zq = jnp.zeros((nb, GQA_R, HEAD_DIM), F32)
    q_pad = jnp.concatenate([jnp.concatenate([q8[:, :GQA_R], zq], -1), jnp.concatenate([zq, q8[:, GQA_R:]], -1)], 1)
    o_c, picks = _sample_select(q_pad, kvc, past // SEL_BLOCK)
    idx = picks[:, :N_SEL]
    o_s = _sample_attend(idx, page_table, q_pad, gates_s, o_c, ks_s, vs_s, kw_s, vw_s, feat(cache_k_win),
                         feat(cache_v_win), feat(cache_k_sel), feat(cache_v_sel))
    out_w_s = (w_out_c, w_out[l][cc:].astype(_MXU_DTYPE)) + out_w[2:]
    y_s = _layer_out(x_sample.reshape(nb, d), cy_s.astype(_MXU_DTYPE), o_s.reshape(nb, -1).astype(_MXU_DTYPE),
                     *out_w_s)
    y_s = y_s.reshape(nb, 1, d)
    new_win = lambda buf, new: jnp.concatenate([buf[l], new.reshape(nb, 1, N_KV, HEAD_DIM)], 1)[:, -wb:][None]
    s_conv = jnp.concatenate([cache_conv[l], u_s[:, None, :]], 1)[:, -(CONV_K - 1):][None]
    s_states = (kv5(kc_s, nb, 1), kv5(vc_s, nb, 1), kv5(ks_s, nb, 1), kv5(vs_s, nb, 1),
                new_win(cache_k_win, kw_s), new_win(cache_v_win, vw_s), s_conv)
    return (y_p, y_s) + p_states + s_states
```
